```python
import math
import jax, jax.numpy as jnp
from jax import lax
import numpy as np

D_MODEL = 1024
BATCH = 8
SEQ = 4096
DEPTH = 1

ATTN_WINDOWS = (128, 512, 2048)
ATTN_DILATIONS = (1, 4, 16)
N_GROUPS = len(ATTN_WINDOWS)
ATTN_HEAD_DIM = 64
ATTN_HEADS_PER_GROUP = D_MODEL // 128
N_ATTN_HEADS = N_GROUPS * ATTN_HEADS_PER_GROUP
ATTN_OUT_WIDTH = ATTN_HEADS_PER_GROUP * ATTN_HEAD_DIM
ATTN_QKV_WIDTH = N_GROUPS * 3 * ATTN_HEADS_PER_GROUP * ATTN_HEAD_DIM
ATTN_BLOCK = 128
NEG_INF = -1e30
NUM_BUCKETS = 32
MAX_EXACT = NUM_BUCKETS // 2
MAX_DISTANCE = 2048
HGRN_HEADS = D_MODEL // 256
HGRN_DK = 128
HGRN_DV = 128
HGRN_WIDTH = HGRN_HEADS * HGRN_DK
HGRN_CHUNK = 32
GATE_WIDTH = 2 * D_MODEL
IN_WIDTH = ATTN_QKV_WIDTH + 4 * HGRN_WIDTH + GATE_WIDTH
D_FF = ((8 * D_MODEL // 3 + 127) // 128) * 128
CONV_WIDTH = 3
NORM_EPS = 1e-6

kernel_name = "hybrid_dilated_attn_hgrn2_convffn"


def rms_norm(x, w):
    xf = x.astype(jnp.float32)
    y = xf * lax.rsqrt(jnp.mean(xf * xf, axis=-1, keepdims=True) + NORM_EPS)
    return (y * w.astype(jnp.float32)).astype(x.dtype)


def t5_bucket(dist):
    n = jnp.maximum(dist, 0)
    nf = jnp.maximum(n, 1).astype(jnp.float32)
    large = MAX_EXACT + (jnp.log(nf / MAX_EXACT) / math.log(MAX_DISTANCE / MAX_EXACT)
                         * (NUM_BUCKETS - MAX_EXACT)).astype(jnp.int32)
    large = jnp.minimum(large, NUM_BUCKETS - 1)
    return jnp.where(n < MAX_EXACT, n, large)


def dilated_window_attention(q, k, v, bias_table, dilation, window):
    Bn, S, H, hd = q.shape
    blk = ATTN_BLOCK
    span = dilation * blk
    Sp = -(-S // span) * span
    pad = ((0, 0), (0, Sp - S), (0, 0), (0, 0))
    q, k, v = jnp.pad(q, pad), jnp.pad(k, pad), jnp.pad(v, pad)
    U = Sp // dilation
    NB = U // blk

    def to_sub(t):
        return t.reshape(Bn, U, dilation, H, hd).transpose(0, 2, 3, 1, 4)

    def band(t):
        ts = to_sub(t)
        prev = jnp.pad(ts, ((0, 0), (0, 0), (0, 0), (blk, 0), (0, 0)))[:, :, :, :U]
        return jnp.concatenate([prev.reshape(Bn, dilation, H, NB, blk, hd),
                                ts.reshape(Bn, dilation, H, NB, blk, hd)], axis=-2)

    qs = to_sub(q).reshape(Bn, dilation, H, NB, blk, hd)
    kb, vb = band(k), band(v)
    s = jnp.einsum('bdhnqe,bdhnke->bdhnqk', qs, kb).astype(jnp.float32) * (hd ** -0.5)

    rel = jnp.arange(blk)[:, None] + blk - jnp.arange(2 * blk)[None, :]
    in_win = (rel >= 0) & (rel <= window // dilation)
    key_ok = (jnp.arange(NB)[:, None] * blk - blk + jnp.arange(2 * blk)[None, :]) >= 0
    mask = in_win[None] & key_ok[:, None, :]
    bias = bias_table[t5_bucket(rel * dilation)].astype(jnp.float32)
    s = s + bias.transpose(2, 0, 1)[:, None]
    s = jnp.where(mask, s, NEG_INF)

    m = jnp.max(s, axis=-1, keepdims=True)
    p = jnp.exp(s - m)
    l = jnp.sum(p, axis=-1, keepdims=True)
    o = jnp.einsum('bdhnqk,bdhnke->bdhnqe', p.astype(vb.dtype), vb).astype(jnp.float32) / l
    lse = (m + jnp.log(l))[..., 0]

    o = o.reshape(Bn, dilation, H, U, hd).transpose(0, 3, 1, 2, 4).reshape(Bn, Sp, H, hd)[:, :S]
    lse = lse.reshape(Bn, dilation, H, U).transpose(0, 3, 1, 2).reshape(Bn, Sp, H)[:, :S]
    return o, lse


def attention_branch(attn_cols, rel_bias):
    Bn, S, _ = attn_cols.shape
    qkv = attn_cols.reshape(Bn, S, N_GROUPS, 3, ATTN_HEADS_PER_GROUP, ATTN_HEAD_DIM)
    outs, lses = [], []
    for g in range(N_GROUPS):
        bias_g = rel_bias[:, g * ATTN_HEADS_PER_GROUP:(g + 1) * ATTN_HEADS_PER_GROUP]
        o, lse = dilated_window_attention(qkv[:, :, g, 0], qkv[:, :, g, 1], qkv[:, :, g, 2],
                                          bias_g, ATTN_DILATIONS[g], ATTN_WINDOWS[g])
        outs.append(o)
        lses.append(lse)
    w = jax.nn.softmax(jnp.stack(lses, axis=0), axis=0)[..., None]
    y = jnp.sum(w * jnp.stack(outs, axis=0), axis=0)
    return y.reshape(Bn, S, ATTN_OUT_WIDTH).astype(attn_cols.dtype)


def hgrn2_chunked(q, k, g, v):
    Bn, H, S, dk = q.shape
    dv = v.shape[-1]
    C = HGRN_CHUNK
    NC = S // C
    q, k, g = (t.reshape(Bn, H, NC, C, dk) for t in (q, k, g))
    v = v.reshape(Bn, H, NC, C, dv)
    G = jnp.cumsum(g, axis=3)
    G_last = G[:, :, :, -1:, :]
    q_t = q * jnp.exp(G)
    k_t = k * jnp.exp(-G)
    k_dec = k * jnp.exp(G_last - G)
    tril = jnp.tril(jnp.ones((C, C), dtype=bool))
    A = jnp.where(tril, jnp.einsum('bhnik,bhnjk->bhnij', q_t, k_t), 0.0)
    o_intra = jnp.einsum('bhnij,bhnjv->bhniv', A, v)

    def step(state, xs):
        qc, kc, glc, vc = xs
        o = jnp.einsum('bhik,bhkv->bhiv', qc, state)
        state = jnp.exp(glc)[:, :, 0, :, None] * state + jnp.einsum('bhjk,bhjv->bhkv', kc, vc)
        return state, o

    xs = tuple(jnp.moveaxis(t, 2, 0) for t in (q_t, k_dec, G_last, v))
    state0 = jnp.zeros((Bn, H, dk, dv), jnp.float32)
    _, o_inter = lax.scan(step, state0, xs)
    o = o_intra + jnp.moveaxis(o_inter, 0, 2)
    return o.reshape(Bn, H, S, dv)


def hgrn2_branch(hg_cols, lb, norm_w):
    Bn, S, _ = hg_cols.shape
    q, f, i, og = jnp.split(hg_cols, 4, axis=-1)

    def heads(t):
        return t.reshape(Bn, S, HGRN_HEADS, -1).transpose(0, 2, 1, 3).astype(jnp.float32)

    lb_h = lb.astype(jnp.float32).reshape(HGRN_HEADS, 1, HGRN_DK)
    q = jax.nn.silu(heads(q))
    f = lb_h + (1.0 - lb_h) * jax.nn.sigmoid(heads(f))
    o = hgrn2_chunked(q, 1.0 - f, jnp.log(f), heads(i))
    o = o * lax.rsqrt(jnp.mean(o * o, axis=-1, keepdims=True) + NORM_EPS) * norm_w.astype(jnp.float32)
    o = o.transpose(0, 2, 1, 3).reshape(Bn, S, HGRN_WIDTH)
    return (o * jax.nn.silu(og.astype(jnp.float32))).astype(hg_cols.dtype)


def conv_ffn(h, w_up, conv_w, conv_b, w_down):
    S = h.shape[1]
    u = h @ w_up
    up = jnp.pad(u, ((0, 0), (CONV_WIDTH - 1, 0), (0, 0)))
    c = conv_b
    for j in range(CONV_WIDTH):
        c = c + conv_w[j] * up[:, j:j + S]
    gate, val = jnp.split(c, 2, axis=-1)
    return (jax.nn.gelu(gate, approximate=False) * val) @ w_down


def setup_inputs(seed: int = 0) -> dict:
    key = jax.random.key(seed)
    ks = jax.random.split(key, 20)
    f32 = jnp.float32

    def nrm(k, shape, scale):
        return jax.random.normal(k, shape, f32) * scale

    def gain(k, shape):
        return 1.0 + 0.05 * jax.random.normal(k, shape, f32)

    return {
        "x": nrm(ks[0], (BATCH, SEQ, D_MODEL), 1.0),
        "pre_mix_norm": gain(ks[1], (DEPTH, D_MODEL)),
        "w_in": nrm(ks[2], (DEPTH, D_MODEL, IN_WIDTH), D_MODEL ** -0.5),
        "rel_bias": nrm(ks[3], (NUM_BUCKETS, N_ATTN_HEADS), 0.5),
        "hgrn_lb_raw": nrm(ks[4], (DEPTH + 1, HGRN_WIDTH), 0.1),
        "hgrn_norm": gain(ks[5], (DEPTH, HGRN_DV)),
        "w_branch_attn": nrm(ks[6], (DEPTH, ATTN_OUT_WIDTH, D_MODEL), ATTN_OUT_WIDTH ** -0.5),
        "w_branch_hgrn": nrm(ks[7], (DEPTH, HGRN_WIDTH, D_MODEL), HGRN_WIDTH ** -0.5),
        "w_out": nrm(ks[8], (DEPTH, D_MODEL, D_MODEL), D_MODEL ** -0.5),
        "post_mix_norm": gain(ks[9], (DEPTH, D_MODEL)),
        "pre_ffn_norm": gain(ks[10], (DEPTH, D_MODEL)),
        "w_up": nrm(ks[11], (DEPTH, D_MODEL, 2 * D_FF), D_MODEL ** -0.5),
        "conv_w": nrm(ks[12], (DEPTH, CONV_WIDTH, 2 * D_FF), CONV_WIDTH ** -0.5),
        "conv_b": nrm(ks[13], (DEPTH, 2 * D_FF), 0.02),
        "w_down": nrm(ks[14], (DEPTH, D_FF, D_MODEL), D_FF ** -0.5),
        "post_ffn_norm": gain(ks[15], (DEPTH, D_MODEL)),
    }


def reference(x, pre_mix_norm, w_in, rel_bias, hgrn_lb_raw, hgrn_norm, w_branch_attn,
              w_branch_hgrn, w_out, post_mix_norm, pre_ffn_norm, w_up, conv_w, conv_b,
              w_down, post_ffn_norm):
    Bn, S, _ = x.shape
    lbs = jnp.cumsum(jax.nn.softmax(hgrn_lb_raw.astype(jnp.float32), axis=0), axis=0)
    split_at = [ATTN_QKV_WIDTH, ATTN_QKV_WIDTH + 4 * HGRN_WIDTH]
    for l in range(DEPTH):
        h = rms_norm(x, pre_mix_norm[l])
        proj = h @ w_in[l]
        attn_cols, hg_cols, gate_cols = jnp.split(proj, split_at, axis=-1)
        y_attn = attention_branch(attn_cols, rel_bias)
        y_hgrn = hgrn2_branch(hg_cols, lbs[l], hgrn_norm[l])
        gates = jax.nn.sigmoid(gate_cols.astype(jnp.float32)).reshape(Bn, S, 2, D_MODEL)
        merged = (gates[:, :, 0] * (y_attn @ w_branch_attn[l])
                  + gates[:, :, 1] * (y_hgrn @ w_branch_hgrn[l])).astype(x.dtype)
        x = x + rms_norm(merged @ w_out[l], post_mix_norm[l])
        h = rms_norm(x, pre_ffn_norm[l])
        x = x + rms_norm(conv_ffn(h, w_up[l], conv_w[l], conv_b[l], w_down[l]), post_ffn_norm[l])
    return x
```

```python
import functools
import math

import numpy as np
import jax
import jax.numpy as jnp
from jax import lax
from jax.experimental import pallas as pl
from jax.experimental.pallas import tpu as pltpu

F32 = jnp.float32
BF16 = jnp.bfloat16

D_MODEL = 1024
ATTN_WINDOWS = (128, 512, 2048)
ATTN_DILATIONS = (1, 4, 16)
N_GROUPS = 3
HEAD_DIM = 64
HEADS_PER_GROUP = 8
ATTN_OUT = HEADS_PER_GROUP * HEAD_DIM
GROUP_QKV = 3 * ATTN_OUT
ATTN_QKV = N_GROUPS * GROUP_QKV
ATTN_BLOCK = 128
NEG_INF = -1e30
NUM_BUCKETS = 32
MAX_EXACT = 16
MAX_DISTANCE = 2048
HGRN_HEADS = 4
HGRN_DK = 128
HGRN_WIDTH = HGRN_HEADS * HGRN_DK
HGRN_CHUNK = 64
GATE_WIDTH = 2 * D_MODEL
IN_WIDTH = ATTN_QKV + 4 * HGRN_WIDTH + GATE_WIDTH
D_FF = 2816
CONV_WIDTH = 3
NORM_EPS = 1e-6

LANES = 128
SUBLANES = 8
VMEM_LIMIT = 56 * 1024 * 1024

NT_DIMS = (((1,), (1,)), ((), ()))
TN_DIMS = (((0,), (0,)), ((), ()))


def _rms(xf, w):
    return xf * lax.rsqrt(jnp.mean(xf * xf, axis=-1, keepdims=True) + NORM_EPS) * w


def _resident(shape):
    nd = len(shape)
    return pl.BlockSpec(shape, lambda *_: (0,) * nd, pipeline_mode=pl.Buffered(1))


def _in_proj_kernel(x_ref, nw_ref, w_ref, lb_ref,
                    qkv_ref, hq_ref, hg_ref, hk_ref, hv_ref, og_ref, gate_ref):
    h = _rms(x_ref[...], nw_ref[...]).astype(BF16)

    def proj(c0, width):
        return jnp.dot(h, w_ref[:, c0:c0 + width], preferred_element_type=F32)

    for c in range(ATTN_QKV // ATTN_OUT):
        acc = proj(c * ATTN_OUT, ATTN_OUT)
        if c % 3 == 0:
            acc = acc * (HEAD_DIM ** -0.5)
        qkv_ref[:, c * ATTN_OUT:(c + 1) * ATTN_OUT] = acc.astype(BF16)

    base = ATTN_QKV
    hq_ref[...] = jax.nn.silu(proj(base, HGRN_WIDTH))
    lb = lb_ref[...]
    f = lb + (1.0 - lb) * jax.nn.sigmoid(proj(base + HGRN_WIDTH, HGRN_WIDTH))
    hg_ref[...] = jnp.log(f)
    hk_ref[...] = 1.0 - f
    hv_ref[...] = proj(base + 2 * HGRN_WIDTH, HGRN_WIDTH)
    og_ref[...] = jax.nn.silu(proj(base + 3 * HGRN_WIDTH, HGRN_WIDTH))

    base = ATTN_QKV + 4 * HGRN_WIDTH
    for c in range(GATE_WIDTH // 512):
        gate_ref[:, c * 512:(c + 1) * 512] = jax.nn.sigmoid(proj(base + c * 512, 512)).astype(BF16)


def _in_proj(x2, nw, w_bf, lb, tm):
    n = x2.shape[0]
    row = lambda width: pl.BlockSpec((tm, width), lambda i: (i, 0))
    hg_shape = jax.ShapeDtypeStruct((n, HGRN_WIDTH), F32)
    return pl.pallas_call(
        _in_proj_kernel,
        grid=(n // tm,),
        in_specs=[row(D_MODEL), _resident((1, D_MODEL)), _resident((D_MODEL, IN_WIDTH)),
                  _resident((1, HGRN_WIDTH))],
        out_specs=[row(ATTN_QKV)] + [row(HGRN_WIDTH)] * 5 + [row(GATE_WIDTH)],
        out_shape=[jax.ShapeDtypeStruct((n, ATTN_QKV), BF16)] + [hg_shape] * 5
                  + [jax.ShapeDtypeStruct((n, GATE_WIDTH), BF16)],
        compiler_params=pltpu.CompilerParams(dimension_semantics=("arbitrary",),
                                             vmem_limit_bytes=VMEM_LIMIT),
        name="in_proj",
    )(x2, nw, w_bf, lb)


def _bucket_table(dilation):
    blk = ATTN_BLOCK
    rel = np.arange(blk)[:, None] + blk - np.arange(2 * blk)[None, :]
    dist = np.maximum(rel * dilation, 0)
    nf = np.maximum(dist, 1).astype(np.float32)
    large = MAX_EXACT + (np.log(nf / np.float32(MAX_EXACT)) / np.float32(math.log(MAX_DISTANCE / MAX_EXACT))
                         * np.float32(NUM_BUCKETS - MAX_EXACT)).astype(np.int32)
    large = np.minimum(large, NUM_BUCKETS - 1)
    bucket = np.where(dist < MAX_EXACT, dist, large)
    in_win = (rel >= 0) & (rel <= blk)
    first = in_win & (np.arange(2 * blk)[None, :] >= blk)
    return np.stack([np.where(in_win, bucket, -1), np.where(first, bucket, -1)]).astype(np.int32)


def _bias_kernel(tab_ref, bucket_ref, bias_ref):
    h = pl.program_id(0)
    for v in range(2):
        bucket = bucket_ref[v]
        acc = jnp.full(bucket.shape, NEG_INF, F32)
        for b in range(NUM_BUCKETS):
            acc = jnp.where(bucket == b, tab_ref[b, h], acc)
        bias_ref[v, 0] = acc


def _attn_bias(rel_bias_g, dilation):
    blk = ATTN_BLOCK
    buckets = jnp.asarray(_bucket_table(dilation))
    return pl.pallas_call(
        _bias_kernel,
        grid=(HEADS_PER_GROUP,),
        in_specs=[pl.BlockSpec(memory_space=pltpu.SMEM),
                  pl.BlockSpec((2, blk, 2 * blk), lambda h: (0, 0, 0))],
        out_specs=pl.BlockSpec((2, 1, blk, 2 * blk), lambda h: (0, h, 0, 0)),
        out_shape=jax.ShapeDtypeStruct((2, HEADS_PER_GROUP, blk, 2 * blk), F32),
        compiler_params=pltpu.CompilerParams(dimension_semantics=("arbitrary",)),
        name="attn_bias",
    )(rel_bias_g, buckets)


def _attn_kernel(q_ref, kp_ref, kc_ref, vp_ref, vc_ref, bias_ref, o_ref, lse_ref, kbuf, vbuf, *, rows):
    blk = ATTN_BLOCK
    n = pl.program_id(2)
    kbuf[0:blk] = kp_ref[...]
    kbuf[blk:blk + rows] = kc_ref[...]
    vbuf[0:blk] = vp_ref[...]
    vbuf[blk:blk + rows] = vc_ref[...]

    lane = lax.broadcasted_iota(jnp.int32, (blk, LANES), 1)
    low = lane < HEAD_DIM
    head_mask = (jnp.where(low, 1.0, 0.0).astype(BF16), jnp.where(low, 0.0, 1.0).astype(BF16))

    for i in range(rows // blk):
        variant = jnp.where(n == 0, 1, 0) if i == 0 else 0
        lse_tile = jnp.zeros((blk, LANES), F32)
        for pair in range(HEADS_PER_GROUP // 2):
            cols = slice(pair * LANES, (pair + 1) * LANES)
            q = q_ref[i * blk:(i + 1) * blk, cols]
            kk = kbuf[i * blk:(i + 2) * blk, cols]
            vv = vbuf[i * blk:(i + 2) * blk, cols]
            outs = []
            for hh in range(2):
                head = 2 * pair + hh
                s = lax.dot_general(q * head_mask[hh], kk, NT_DIMS, preferred_element_type=F32)
                s = s + bias_ref[variant, head]
                m = jnp.max(s, axis=-1, keepdims=True)
                p = jnp.exp(s - m)
                l = jnp.sum(p, axis=-1, keepdims=True)
                pv = jnp.dot(p.astype(BF16), vv, preferred_element_type=F32)
                outs.append(pv / l)
                lse_tile = jnp.where(lane == head, m + jnp.log(l), lse_tile)
            o_ref[i * blk:(i + 1) * blk, cols] = jnp.where(low, outs[0], outs[1])
        lse_ref[i * blk:(i + 1) * blk, :] = lse_tile


def _attention_group(qkv, bias, g, rows):
    bsz, seq, _ = qkv.shape
    d = ATTN_DILATIONS[g]
    u = seq // d
    blk = ATTN_BLOCK
    rows = min(rows, u)
    nb = u // rows
    per_tok = ATTN_QKV // ATTN_OUT
    view = qkv.reshape(bsz, u, d * ATTN_QKV)

    def cur(j):
        return pl.BlockSpec((None, rows, ATTN_OUT), lambda b, r, n: (b, n, r * per_tok + 3 * g + j))

    def prev(j):
        return pl.BlockSpec((None, blk, ATTN_OUT),
                            lambda b, r, n: (b, jnp.maximum(n * (rows // blk) - 1, 0), r * per_tok + 3 * g + j))

    o, lse = pl.pallas_call(
        functools.partial(_attn_kernel, rows=rows),
        grid=(bsz, d, nb),
        in_specs=[cur(0), prev(1), cur(1), prev(2), cur(2),
                  _resident((2, HEADS_PER_GROUP, blk, 2 * blk))],
        out_specs=[pl.BlockSpec((None, rows, ATTN_OUT), lambda b, r, n: (b, n, r)),
                   pl.BlockSpec((None, rows, LANES), lambda b, r, n: (b, n, r))],
        out_shape=[jax.ShapeDtypeStruct((bsz, u, d * ATTN_OUT), F32),
                   jax.ShapeDtypeStruct((bsz, u, d * LANES), F32)],
        scratch_shapes=[pltpu.VMEM((blk + rows, ATTN_OUT), BF16), pltpu.VMEM((blk + rows, ATTN_OUT), BF16)],
        compiler_params=pltpu.CompilerParams(dimension_semantics=("arbitrary",) * 3,
                                             vmem_limit_bytes=VMEM_LIMIT),
        name=f"attn_g{g}",
    )(view, view, view, view, view, bias)
    return o.reshape(bsz * seq, ATTN_OUT), lse.reshape(bsz * seq, LANES)


def _hgrn_kernel(q_ref, g_ref, k_ref, v_ref, og_ref, nw_ref, o_ref, st_ref, *, rows):
    c_len = HGRN_CHUNK

    @pl.when(pl.program_id(2) == 0)
    def _():
        st_ref[...] = jnp.zeros_like(st_ref)

    r_i = lax.broadcasted_iota(jnp.int32, (c_len, c_len), 0)
    c_i = lax.broadcasted_iota(jnp.int32, (c_len, c_len), 1)
    tril = r_i >= c_i
    ones_tril = jnp.where(tril, 1.0, 0.0).astype(BF16)
    nw = nw_ref[...]

    def chunk(c, carry):
        sl = pl.ds(pl.multiple_of(c * c_len, c_len), c_len)
        g = g_ref[sl, :]
        g1 = g.astype(BF16)
        r1 = g - g1.astype(F32)
        g2 = r1.astype(BF16)
        g3 = (r1 - g2.astype(F32)).astype(BF16)
        cum = (jnp.dot(ones_tril, g1, preferred_element_type=F32)
               + jnp.dot(ones_tril, g2, preferred_element_type=F32)
               + jnp.dot(ones_tril, g3, preferred_element_type=F32))
        mid = cum[c_len // 2 - 1:c_len // 2, :]
        last = cum[c_len - 1:c_len, :]
        q = q_ref[sl, :]
        k = k_ref[sl, :]
        v = v_ref[sl, :].astype(BF16)
        q_t = (q * jnp.exp(cum - mid)).astype(BF16)
        k_t = (k * jnp.exp(mid - cum)).astype(BF16)
        a = lax.dot_general(q_t, k_t, NT_DIMS, preferred_element_type=F32)
        a = jnp.where(tril, a, 0.0).astype(BF16)
        state_t = st_ref[...]
        q_d = (q * jnp.exp(cum)).astype(BF16)
        o = (jnp.dot(a, v, preferred_element_type=F32)
             + lax.dot_general(q_d, state_t.astype(BF16), NT_DIMS, preferred_element_type=F32))
        k_d = (k * jnp.exp(last - cum)).astype(BF16)
        st_ref[...] = state_t * jnp.exp(last) + lax.dot_general(v, k_d, TN_DIMS, preferred_element_type=F32)
        y = _rms(o, nw) * og_ref[sl, :]
        o_ref[sl, :] = y.astype(BF16)
        return carry

    lax.fori_loop(0, rows // c_len, chunk, 0)


def _hgrn(hq, hg, hk, hv, og, nw, bsz, seq, rows):
    n = bsz * seq
    tps = seq // rows
    spec = pl.BlockSpec((rows, HGRN_DK), lambda b, h, t: (b * tps + t, h))
    return pl.pallas_call(
        functools.partial(_hgrn_kernel, rows=rows),
        grid=(bsz, HGRN_HEADS, tps),
        in_specs=[spec] * 5 + [_resident((1, HGRN_DK))],
        out_specs=spec,
        out_shape=jax.ShapeDtypeStruct((n, HGRN_WIDTH), BF16),
        scratch_shapes=[pltpu.VMEM((HGRN_DK, HGRN_DK), F32)],
        compiler_params=pltpu.CompilerParams(dimension_semantics=("arbitrary",) * 3,
                                             vmem_limit_bytes=VMEM_LIMIT),
        name="hgrn2",
    )(hq, hg, hk, hv, og, nw)


def _merge_kernel(o1_ref, o2_ref, o3_ref, l1_ref, l2_ref, l3_ref, yh_ref, gate_ref, x_ref,
                  expand_ref, wa_ref, wh_ref, wo_ref, nw_ref, out_ref):
    lses = [l1_ref[...], l2_ref[...], l3_ref[...]]
    mx = jnp.maximum(jnp.maximum(lses[0], lses[1]), lses[2])
    es = [jnp.exp(l - mx) for l in lses]
    den = es[0] + es[1] + es[2]
    expand = expand_ref[...]
    y = None
    for e, o_ref in zip(es, (o1_ref, o2_ref, o3_ref)):
        w = e / den
        w_hi = w.astype(BF16)
        w_lo = (w - w_hi.astype(F32)).astype(BF16)
        w_full = (jnp.dot(w_hi, expand, preferred_element_type=F32)
                  + jnp.dot(w_lo, expand, preferred_element_type=F32))
        term = w_full * o_ref[...]
        y = term if y is None else y + term
    pa = jnp.dot(y.astype(BF16), wa_ref[...], preferred_element_type=F32)
    ph = jnp.dot(yh_ref[...], wh_ref[...], preferred_element_type=F32)
    merged = (gate_ref[:, 0:D_MODEL].astype(F32) * pa + gate_ref[:, D_MODEL:GATE_WIDTH].astype(F32) * ph)
    z = jnp.dot(merged.astype(BF16), wo_ref[...], preferred_element_type=F32)
    out_ref[...] = x_ref[...] + _rms(z, nw_ref[...])


def _merge(os, lses, yh, gates, x2, wa, wh, wo, nw, tm):
    n = x2.shape[0]
    row = lambda width: pl.BlockSpec((tm, width), lambda i: (i, 0))
    head_of_col = np.arange(ATTN_OUT) // HEAD_DIM
    expand = jnp.asarray(np.arange(LANES)[:, None] == head_of_col[None, :], dtype=BF16)
    return pl.pallas_call(
        _merge_kernel,
        grid=(n // tm,),
        in_specs=[row(ATTN_OUT)] * 3 + [row(LANES)] * 3 + [row(HGRN_WIDTH), row(GATE_WIDTH), row(D_MODEL),
                  _resident((LANES, ATTN_OUT)), _resident((ATTN_OUT, D_MODEL)),
                  _resident((HGRN_WIDTH, D_MODEL)), _resident((D_MODEL, D_MODEL)), _resident((1, D_MODEL))],
        out_specs=row(D_MODEL),
        out_shape=jax.ShapeDtypeStruct((n, D_MODEL), F32),
        compiler_params=pltpu.CompilerParams(dimension_semantics=("arbitrary",),
                                             vmem_limit_bytes=VMEM_LIMIT),
        name="merge_out",
    )(*os, *lses, yh, gates, x2, expand, wa, wh, wo, nw)


FFN_CHUNK = 256
HALO = SUBLANES


def _ffn_kernel(x_ref, nw1_ref, wup_ref, cw_ref, cb_ref, wdn_ref, nw2_ref, out_ref,
                ubuf, carry, act, *, tm, tiles_per_seq):
    first = (pl.program_id(0) % tiles_per_seq) == 0

    @pl.when(pl.program_id(0) == 0)
    def _():
        carry[...] = jnp.zeros_like(carry)

    xf = x_ref[...]
    h = _rms(xf, nw1_ref[...]).astype(BF16)
    for c in range(D_FF // FFN_CHUNK):
        conv = []
        for half in range(2):
            cols = slice(half * D_FF + c * FFN_CHUNK, half * D_FF + (c + 1) * FFN_CHUNK)
            slot = 2 * c + half
            u = jnp.dot(h, wup_ref[:, cols], preferred_element_type=F32)
            ubuf[half, 0:HALO, :] = jnp.where(first, 0.0, carry[slot])
            ubuf[half, HALO:HALO + tm, :] = u
            carry[slot] = u[tm - HALO:tm, :]
            w = cw_ref[:, cols]
            conv.append(cb_ref[:, cols]
                        + w[0:1] * ubuf[half, HALO - 2:HALO - 2 + tm, :]
                        + w[1:2] * ubuf[half, HALO - 1:HALO - 1 + tm, :]
                        + w[2:3] * u)
        gate, val = conv
        gelu = 0.5 * gate * (1.0 + lax.erf(gate * (2.0 ** -0.5)))
        act[:, c * FFN_CHUNK:(c + 1) * FFN_CHUNK] = (gelu * val).astype(BF16)
    y = jnp.dot(act[...], wdn_ref[...], preferred_element_type=F32)
    out_ref[...] = xf + _rms(y, nw2_ref[...])


def _ffn(x2, nw1, wup, cw, cb, wdn, nw2, seq, tm):
    n = x2.shape[0]
    row = pl.BlockSpec((tm, D_MODEL), lambda i: (i, 0))
    return pl.pallas_call(
        functools.partial(_ffn_kernel, tm=tm, tiles_per_seq=seq // tm),
        grid=(n // tm,),
        in_specs=[row, _resident((1, D_MODEL)), _resident((D_MODEL, 2 * D_FF)),
                  _resident((CONV_WIDTH, 2 * D_FF)), _resident((1, 2 * D_FF)),
                  _resident((D_FF, D_MODEL)), _resident((1, D_MODEL))],
        out_specs=row,
        out_shape=jax.ShapeDtypeStruct((n, D_MODEL), F32),
        scratch_shapes=[pltpu.VMEM((2, HALO + tm, FFN_CHUNK), F32),
                        pltpu.VMEM((2 * (D_FF // FFN_CHUNK), HALO, FFN_CHUNK), F32),
                        pltpu.VMEM((tm, D_FF), BF16)],
        compiler_params=pltpu.CompilerParams(dimension_semantics=("arbitrary",),
                                             vmem_limit_bytes=VMEM_LIMIT),
        name="conv_ffn",
    )(x2, nw1, wup, cw, cb, wdn, nw2)


def kernel(x, pre_mix_norm, w_in, rel_bias, hgrn_lb_raw, hgrn_norm, w_branch_attn, w_branch_hgrn,
           w_out, post_mix_norm, pre_ffn_norm, w_up, conv_w, conv_b, w_down, post_ffn_norm):
    bsz, seq, _ = x.shape
    depth = w_in.shape[0]
    n = bsz * seq
    lbs = jnp.cumsum(jax.nn.softmax(hgrn_lb_raw.astype(F32), axis=0), axis=0)
    x2 = x.reshape(n, D_MODEL)
    for l in range(depth):
        qkv, hq, hg, hk, hv, og, gates = _in_proj(
            x2, pre_mix_norm[l][None], w_in[l].astype(BF16), lbs[l][None], tm=256)
        qkv3 = qkv.reshape(bsz, seq, ATTN_QKV)
        os, lses = [], []
        for g in range(N_GROUPS):
            bias = _attn_bias(rel_bias[:, g * HEADS_PER_GROUP:(g + 1) * HEADS_PER_GROUP], ATTN_DILATIONS[g])
            o, lse = _attention_group(qkv3, bias, g, rows=256)
            os.append(o)
            lses.append(lse)
        yh = _hgrn(hq, hg, hk, hv, og, hgrn_norm[l][None], bsz, seq, rows=512)
        x2 = _merge(os, lses, yh, gates, x2, w_branch_attn[l].astype(BF16), w_branch_hgrn[l].astype(BF16),
                    w_out[l].astype(BF16), post_mix_norm[l][None], tm=512)
        x2 = _ffn(x2, pre_ffn_norm[l][None], w_up[l].astype(BF16), conv_w[l], conv_b[l][None],
                  w_down[l].astype(BF16), post_ffn_norm[l][None], seq, tm=512)
    return x2.reshape(bsz, seq, D_MODEL)
```

```python
import functools
import math

import numpy as np
import jax
import jax.numpy as jnp
from jax import lax
from jax.experimental import pallas as pl
from jax.experimental.pallas import tpu as pltpu

F32 = jnp.float32
BF16 = jnp.bfloat16

D_MODEL = 1024
ATTN_WINDOWS = (128, 512, 2048)
ATTN_DILATIONS = (1, 4, 16)
N_GROUPS = 3
HEAD_DIM = 64
HEADS_PER_GROUP = 8
ATTN_OUT = HEADS_PER_GROUP * HEAD_DIM
GROUP_QKV = 3 * ATTN_OUT
ATTN_QKV = N_GROUPS * GROUP_QKV
ATTN_BLOCK = 128
NEG_INF = -1e30
NUM_BUCKETS = 32
MAX_EXACT = 16
MAX_DISTANCE = 2048
HGRN_HEADS = 4
HGRN_DK = 128
HGRN_WIDTH = HGRN_HEADS * HGRN_DK
HGRN_CHUNK = 64
GATE_WIDTH = 2 * D_MODEL
IN_WIDTH = ATTN_QKV + 4 * HGRN_WIDTH + GATE_WIDTH
D_FF = 2816
CONV_WIDTH = 3
NORM_EPS = 1e-6

LANES = 128
SUBLANES = 8
VMEM_LIMIT = 56 * 1024 * 1024

NT_DIMS = (((1,), (1,)), ((), ()))
TN_DIMS = (((0,), (0,)), ((), ()))


def _rms(xf, w):
    return xf * lax.rsqrt(jnp.mean(xf * xf, axis=-1, keepdims=True) + NORM_EPS) * w


def _resident(shape):
    nd = len(shape)
    return pl.BlockSpec(shape, lambda *_: (0,) * nd, pipeline_mode=pl.Buffered(1))


def _in_proj_kernel(x_ref, nw_ref, w_ref, lb_ref,
                    qkv0_ref, qkv1_ref, qkv2_ref, hq_ref, hg_ref, hk_ref, hv_ref, og_ref, gate_ref,
                    perm, *, tm):
    h = _rms(x_ref[...], nw_ref[...]).astype(BF16)

    def proj(c0, width):
        return jnp.dot(h, w_ref[:, c0:c0 + width], preferred_element_type=F32)

    slabs = ATTN_OUT // LANES
    for g, out_ref in enumerate((qkv0_ref, qkv1_ref, qkv2_ref)):
        d = ATTN_DILATIONS[g]
        for j in range(3):
            c = 3 * g + j
            acc = proj(c * ATTN_OUT, ATTN_OUT)
            if j == 0:
                acc = acc * (HEAD_DIM ** -0.5)
            if d == 1:
                out_ref[0, :, j * ATTN_OUT:(j + 1) * ATTN_OUT] = acc.astype(BF16)
                continue
            buf = c % 2
            for s in range(slabs):
                perm[buf, s] = acc[:, s * LANES:(s + 1) * LANES]
            for r in range(d):
                for s in range(slabs):
                    rows = perm[buf, s, pl.ds(r, tm // d, stride=d), :]
                    out_ref[r, :, j * ATTN_OUT + s * LANES:j * ATTN_OUT + (s + 1) * LANES] = rows.astype(BF16)

    base = ATTN_QKV
    hq_ref[...] = jax.nn.silu(proj(base, HGRN_WIDTH))
    lb = lb_ref[...]
    f = lb + (1.0 - lb) * jax.nn.sigmoid(proj(base + HGRN_WIDTH, HGRN_WIDTH))
    hg_ref[...] = jnp.log(f)
    hk_ref[...] = 1.0 - f
    hv_ref[...] = proj(base + 2 * HGRN_WIDTH, HGRN_WIDTH)
    og_ref[...] = jax.nn.silu(proj(base + 3 * HGRN_WIDTH, HGRN_WIDTH))

    base = ATTN_QKV + 4 * HGRN_WIDTH
    for c in range(GATE_WIDTH // 512):
        gate_ref[:, c * 512:(c + 1) * 512] = jax.nn.sigmoid(proj(base + c * 512, 512)).astype(BF16)


def _in_proj(x2, nw, w_bf, lb, bsz, seq, tm):
    n = x2.shape[0]
    tps = seq // tm
    row = lambda width: pl.BlockSpec((tm, width), lambda i: (i, 0))
    hg_shape = jax.ShapeDtypeStruct((n, HGRN_WIDTH), F32)
    qkv_specs = [pl.BlockSpec((None, d, tm // d, GROUP_QKV), lambda i: (i // tps, 0, i % tps, 0))
                 for d in ATTN_DILATIONS]
    qkv_shapes = [jax.ShapeDtypeStruct((bsz, d, seq // d, GROUP_QKV), BF16) for d in ATTN_DILATIONS]
    return pl.pallas_call(
        functools.partial(_in_proj_kernel, tm=tm),
        grid=(n // tm,),
        in_specs=[row(D_MODEL), _resident((1, D_MODEL)), _resident((D_MODEL, IN_WIDTH)),
                  _resident((1, HGRN_WIDTH))],
        out_specs=qkv_specs + [row(HGRN_WIDTH)] * 5 + [row(GATE_WIDTH)],
        out_shape=qkv_shapes + [hg_shape] * 5 + [jax.ShapeDtypeStruct((n, GATE_WIDTH), BF16)],
        scratch_shapes=[pltpu.VMEM((2, ATTN_OUT // LANES, tm, LANES), F32)],
        compiler_params=pltpu.CompilerParams(dimension_semantics=("arbitrary",),
                                             vmem_limit_bytes=VMEM_LIMIT),
        name="in_proj",
    )(x2, nw, w_bf, lb)


def _bucket_table(dilation):
    blk = ATTN_BLOCK
    rel = np.arange(blk)[:, None] + blk - np.arange(2 * blk)[None, :]
    dist = np.maximum(rel * dilation, 0)
    nf = np.maximum(dist, 1).astype(np.float32)
    large = MAX_EXACT + (np.log(nf / np.float32(MAX_EXACT)) / np.float32(math.log(MAX_DISTANCE / MAX_EXACT))
                         * np.float32(NUM_BUCKETS - MAX_EXACT)).astype(np.int32)
    large = np.minimum(large, NUM_BUCKETS - 1)
    bucket = np.where(dist < MAX_EXACT, dist, large)
    in_win = (rel >= 0) & (rel <= blk)
    first = in_win & (np.arange(2 * blk)[None, :] >= blk)
    return np.stack([np.where(in_win, bucket, -1), np.where(first, bucket, -1)]).astype(np.int32)


def _bias_kernel(tab_ref, bucket_ref, bias_ref):
    h = pl.program_id(0)
    for v in range(2):
        bucket = bucket_ref[v]
        acc = jnp.full(bucket.shape, NEG_INF, F32)
        for b in range(NUM_BUCKETS):
            acc = jnp.where(bucket == b, tab_ref[b, h], acc)
        bias_ref[v, 0] = acc


def _attn_bias(rel_bias_g, dilation):
    blk = ATTN_BLOCK
    buckets = jnp.asarray(_bucket_table(dilation))
    return pl.pallas_call(
        _bias_kernel,
        grid=(HEADS_PER_GROUP,),
        in_specs=[pl.BlockSpec(memory_space=pltpu.SMEM),
                  pl.BlockSpec((2, blk, 2 * blk), lambda h: (0, 0, 0))],
        out_specs=pl.BlockSpec((2, 1, blk, 2 * blk), lambda h: (0, h, 0, 0)),
        out_shape=jax.ShapeDtypeStruct((2, HEADS_PER_GROUP, blk, 2 * blk), F32),
        compiler_params=pltpu.CompilerParams(dimension_semantics=("arbitrary",)),
        name="attn_bias",
    )(rel_bias_g, buckets)


def _attn_kernel(q_ref, kp_ref, kc_ref, vp_ref, vc_ref, bias_ref, o_ref, lse_ref, kbuf, vbuf, *, rows):
    blk = ATTN_BLOCK
    n = pl.program_id(2)
    kbuf[0:blk] = kp_ref[...]
    kbuf[blk:blk + rows] = kc_ref[...]
    vbuf[0:blk] = vp_ref[...]
    vbuf[blk:blk + rows] = vc_ref[...]

    lane = lax.broadcasted_iota(jnp.int32, (blk, LANES), 1)
    low = lane < HEAD_DIM
    head_mask = (jnp.where(low, 1.0, 0.0).astype(BF16), jnp.where(low, 0.0, 1.0).astype(BF16))

    for i in range(rows // blk):
        variant = jnp.where(n == 0, 1, 0) if i == 0 else 0
        lse_tile = jnp.zeros((blk, LANES), F32)
        for pair in range(HEADS_PER_GROUP // 2):
            cols = slice(pair * LANES, (pair + 1) * LANES)
            q = q_ref[i * blk:(i + 1) * blk, cols]
            kk = kbuf[i * blk:(i + 2) * blk, cols]
            vv = vbuf[i * blk:(i + 2) * blk, cols]
            outs = []
            for hh in range(2):
                head = 2 * pair + hh
                s = lax.dot_general(q * head_mask[hh], kk, NT_DIMS, preferred_element_type=F32)
                s = s + bias_ref[variant, head]
                m = jnp.max(s, axis=-1, keepdims=True)
                p = jnp.exp(s - m)
                l = jnp.sum(p, axis=-1, keepdims=True)
                pv = jnp.dot(p.astype(BF16), vv, preferred_element_type=F32)
                outs.append(pv / l)
                lse_tile = jnp.where(lane == head, m + jnp.log(l), lse_tile)
            o_ref[i * blk:(i + 1) * blk, cols] = jnp.where(low, outs[0], outs[1])
        lse_ref[i * blk:(i + 1) * blk, :] = lse_tile


def _attention_group(qkv, bias, g, rows):
    bsz, d, u, _ = qkv.shape
    blk = ATTN_BLOCK
    rows = min(rows, u)
    nb = u // rows

    def cur(j):
        return pl.BlockSpec((None, None, rows, ATTN_OUT), lambda b, r, n: (b, r, n, j))

    def prev(j):
        return pl.BlockSpec((None, None, blk, ATTN_OUT),
                            lambda b, r, n: (b, r, jnp.maximum(n * (rows // blk) - 1, 0), j))

    return pl.pallas_call(
        functools.partial(_attn_kernel, rows=rows),
        grid=(bsz, d, nb),
        in_specs=[cur(0), prev(1), cur(1), prev(2), cur(2),
                  _resident((2, HEADS_PER_GROUP, blk, 2 * blk))],
        out_specs=[pl.BlockSpec((None, None, rows, ATTN_OUT), lambda b, r, n: (b, r, n, 0)),
                   pl.BlockSpec((None, None, rows, LANES), lambda b, r, n: (b, r, n, 0))],
        out_shape=[jax.ShapeDtypeStruct((bsz, d, u, ATTN_OUT), F32),
                   jax.ShapeDtypeStruct((bsz, d, u, LANES), F32)],
        scratch_shapes=[pltpu.VMEM((blk + rows, ATTN_OUT), BF16), pltpu.VMEM((blk + rows, ATTN_OUT), BF16)],
        compiler_params=pltpu.CompilerParams(dimension_semantics=("arbitrary",) * 3,
                                             vmem_limit_bytes=VMEM_LIMIT),
        name=f"attn_g{g}",
    )(qkv, qkv, qkv, qkv, qkv, bias)


def _hgrn_kernel(q_ref, g_ref, k_ref, v_ref, og_ref, nw_ref, o_ref, st_ref, *, rows):
    c_len = HGRN_CHUNK

    @pl.when(pl.program_id(1) == 0)
    def _():
        st_ref[...] = jnp.zeros_like(st_ref)

    r_i = lax.broadcasted_iota(jnp.int32, (c_len, c_len), 0)
    c_i = lax.broadcasted_iota(jnp.int32, (c_len, c_len), 1)
    tril = r_i >= c_i
    ones_tril = jnp.where(tril, 1.0, 0.0).astype(BF16)
    nw = nw_ref[...]

    for c in range(rows // c_len):
        sl = slice(c * c_len, (c + 1) * c_len)
        g = g_ref[sl, :]
        g1 = g.astype(BF16)
        r1 = g - g1.astype(F32)
        g2 = r1.astype(BF16)
        g3 = (r1 - g2.astype(F32)).astype(BF16)
        cum_all = (jnp.dot(ones_tril, g1, preferred_element_type=F32)
                   + jnp.dot(ones_tril, g2, preferred_element_type=F32)
                   + jnp.dot(ones_tril, g3, preferred_element_type=F32))
        for hd in range(HGRN_HEADS):
            cols = slice(hd * HGRN_DK, (hd + 1) * HGRN_DK)
            cum = cum_all[:, cols]
            mid = cum[c_len // 2 - 1:c_len // 2, :]
            last = cum[c_len - 1:c_len, :]
            q = q_ref[sl, cols]
            k = k_ref[sl, cols]
            v = v_ref[sl, cols].astype(BF16)
            q_t = (q * jnp.exp(cum - mid)).astype(BF16)
            k_t = (k * jnp.exp(mid - cum)).astype(BF16)
            a = lax.dot_general(q_t, k_t, NT_DIMS, preferred_element_type=F32)
            a = jnp.where(tril, a, 0.0).astype(BF16)
            state_t = st_ref[hd]
            q_d = (q * jnp.exp(cum)).astype(BF16)
            o = (jnp.dot(a, v, preferred_element_type=F32)
                 + lax.dot_general(q_d, state_t.astype(BF16), NT_DIMS, preferred_element_type=F32))
            k_d = (k * jnp.exp(last - cum)).astype(BF16)
            st_ref[hd] = state_t * jnp.exp(last) + lax.dot_general(v, k_d, TN_DIMS, preferred_element_type=F32)
            y = _rms(o, nw) * og_ref[sl, cols]
            o_ref[sl, cols] = y.astype(BF16)


def _hgrn(hq, hg, hk, hv, og, nw, bsz, seq, rows):
    n = bsz * seq
    tps = seq // rows
    spec = pl.BlockSpec((rows, HGRN_WIDTH), lambda b, t: (b * tps + t, 0))
    return pl.pallas_call(
        functools.partial(_hgrn_kernel, rows=rows),
        grid=(bsz, tps),
        in_specs=[spec] * 5 + [_resident((1, HGRN_DK))],
        out_specs=spec,
        out_shape=jax.ShapeDtypeStruct((n, HGRN_WIDTH), BF16),
        scratch_shapes=[pltpu.VMEM((HGRN_HEADS, HGRN_DK, HGRN_DK), F32)],
        compiler_params=pltpu.CompilerParams(dimension_semantics=("arbitrary",) * 2,
                                             vmem_limit_bytes=VMEM_LIMIT),
        name="hgrn2",
    )(hq, hg, hk, hv, og, nw)


def _merge_kernel(o1_ref, o2_ref, o3_ref, l1_ref, l2_ref, l3_ref, yh_ref, gate_ref, x_ref,
                  expand_ref, wa_ref, wh_ref, wo_ref, nw_ref, out_ref, o_tok, l_tok, *, tm):
    slabs = ATTN_OUT // LANES
    os = [o1_ref[0]]
    lses = [l1_ref[0]]
    for g, (o_ref, l_ref) in enumerate(((o2_ref, l2_ref), (o3_ref, l3_ref))):
        d = ATTN_DILATIONS[g + 1]
        for r in range(d):
            tok_rows = pl.ds(r, tm // d, stride=d)
            l_tok[g, tok_rows, :] = l_ref[r]
            for s in range(slabs):
                o_tok[g, s, tok_rows, :] = o_ref[r, :, s * LANES:(s + 1) * LANES]
        os.append(jnp.concatenate([o_tok[g, s] for s in range(slabs)], axis=1))
        lses.append(l_tok[g])
    mx = jnp.maximum(jnp.maximum(lses[0], lses[1]), lses[2])
    es = [jnp.exp(l - mx) for l in lses]
    den = es[0] + es[1] + es[2]
    expand = expand_ref[...]
    y = None
    for e, o in zip(es, os):
        w = e / den
        w_hi = w.astype(BF16)
        w_lo = (w - w_hi.astype(F32)).astype(BF16)
        w_full = (jnp.dot(w_hi, expand, preferred_element_type=F32)
                  + jnp.dot(w_lo, expand, preferred_element_type=F32))
        term = w_full * o
        y = term if y is None else y + term
    pa = jnp.dot(y.astype(BF16), wa_ref[...], preferred_element_type=F32)
    ph = jnp.dot(yh_ref[...], wh_ref[...], preferred_element_type=F32)
    merged = (gate_ref[:, 0:D_MODEL].astype(F32) * pa + gate_ref[:, D_MODEL:GATE_WIDTH].astype(F32) * ph)
    z = jnp.dot(merged.astype(BF16), wo_ref[...], preferred_element_type=F32)
    out_ref[...] = x_ref[...] + _rms(z, nw_ref[...])


def _merge(os, lses, yh, gates, x2, wa, wh, wo, nw, seq, tm):
    n = x2.shape[0]
    tps = seq // tm
    row = lambda width: pl.BlockSpec((tm, width), lambda i: (i, 0))
    grouped = lambda width: [pl.BlockSpec((None, d, tm // d, width), lambda i: (i // tps, 0, i % tps, 0))
                             for d in ATTN_DILATIONS]
    head_of_col = np.arange(ATTN_OUT) // HEAD_DIM
    expand = jnp.asarray(np.arange(LANES)[:, None] == head_of_col[None, :], dtype=BF16)
    return pl.pallas_call(
        functools.partial(_merge_kernel, tm=tm),
        grid=(n // tm,),
        in_specs=grouped(ATTN_OUT) + grouped(LANES) + [row(HGRN_WIDTH), row(GATE_WIDTH), row(D_MODEL),
                  _resident((LANES, ATTN_OUT)), _resident((ATTN_OUT, D_MODEL)),
                  _resident((HGRN_WIDTH, D_MODEL)), _resident((D_MODEL, D_MODEL)), _resident((1, D_MODEL))],
        out_specs=row(D_MODEL),
        out_shape=jax.ShapeDtypeStruct((n, D_MODEL), F32),
        scratch_shapes=[pltpu.VMEM((N_GROUPS - 1, ATTN_OUT // LANES, tm, LANES), F32),
                        pltpu.VMEM((N_GROUPS - 1, tm, LANES), F32)],
        compiler_params=pltpu.CompilerParams(dimension_semantics=("arbitrary",),
                                             vmem_limit_bytes=VMEM_LIMIT),
        name="merge_out",
    )(*os, *lses, yh, gates, x2, expand, wa, wh, wo, nw)


FFN_CHUNK = 256
HALO = SUBLANES


def _ffn_kernel(x_ref, nw1_ref, wup_ref, cw_ref, cb_ref, wdn_ref, nw2_ref, out_ref,
                ubuf, carry, act, *, tm, tiles_per_seq):
    first = (pl.program_id(0) % tiles_per_seq) == 0

    @pl.when(pl.program_id(0) == 0)
    def _():
        carry[...] = jnp.zeros_like(carry)

    xf = x_ref[...]
    h = _rms(xf, nw1_ref[...]).astype(BF16)
    for c in range(D_FF // FFN_CHUNK):
        conv = []
        for half in range(2):
            cols = slice(half * D_FF + c * FFN_CHUNK, half * D_FF + (c + 1) * FFN_CHUNK)
            slot = 2 * c + half
            u = jnp.dot(h, wup_ref[:, cols], preferred_element_type=F32)
            ubuf[half, 0:HALO, :] = jnp.where(first, 0.0, carry[slot])
            ubuf[half, HALO:HALO + tm, :] = u
            carry[slot] = u[tm - HALO:tm, :]
            w = cw_ref[:, cols]
            conv.append(cb_ref[:, cols]
                        + w[0:1] * ubuf[half, HALO - 2:HALO - 2 + tm, :]
                        + w[1:2] * ubuf[half, HALO - 1:HALO - 1 + tm, :]
                        + w[2:3] * u)
        gate, val = conv
        gelu = 0.5 * gate * (1.0 + lax.erf(gate * (2.0 ** -0.5)))
        act[:, c * FFN_CHUNK:(c + 1) * FFN_CHUNK] = (gelu * val).astype(BF16)
    y = jnp.dot(act[...], wdn_ref[...], preferred_element_type=F32)
    out_ref[...] = xf + _rms(y, nw2_ref[...])


def _ffn(x2, nw1, wup, cw, cb, wdn, nw2, seq, tm):
    n = x2.shape[0]
    row = pl.BlockSpec((tm, D_MODEL), lambda i: (i, 0))
    return pl.pallas_call(
        functools.partial(_ffn_kernel, tm=tm, tiles_per_seq=seq // tm),
        grid=(n // tm,),
        in_specs=[row, _resident((1, D_MODEL)), _resident((D_MODEL, 2 * D_FF)),
                  _resident((CONV_WIDTH, 2 * D_FF)), _resident((1, 2 * D_FF)),
                  _resident((D_FF, D_MODEL)), _resident((1, D_MODEL))],
        out_specs=row,
        out_shape=jax.ShapeDtypeStruct((n, D_MODEL), F32),
        scratch_shapes=[pltpu.VMEM((2, HALO + tm, FFN_CHUNK), F32),
                        pltpu.VMEM((2 * (D_FF // FFN_CHUNK), HALO, FFN_CHUNK), F32),
                        pltpu.VMEM((tm, D_FF), BF16)],
        compiler_params=pltpu.CompilerParams(dimension_semantics=("arbitrary",),
                                             vmem_limit_bytes=VMEM_LIMIT),
        name="conv_ffn",
    )(x2, nw1, wup, cw, cb, wdn, nw2)


def kernel(x, pre_mix_norm, w_in, rel_bias, hgrn_lb_raw, hgrn_norm, w_branch_attn, w_branch_hgrn,
           w_out, post_mix_norm, pre_ffn_norm, w_up, conv_w, conv_b, w_down, post_ffn_norm):
    bsz, seq, _ = x.shape
    depth = w_in.shape[0]
    n = bsz * seq
    lbs = jnp.cumsum(jax.nn.softmax(hgrn_lb_raw.astype(F32), axis=0), axis=0)
    x2 = x.reshape(n, D_MODEL)
    for l in range(depth):
        *qkvs, hq, hg, hk, hv, og, gates = _in_proj(
            x2, pre_mix_norm[l][None], w_in[l].astype(BF16), lbs[l][None], bsz, seq, tm=256)
        os, lses = [], []
        for g in range(N_GROUPS):
            bias = _attn_bias(rel_bias[:, g * HEADS_PER_GROUP:(g + 1) * HEADS_PER_GROUP], ATTN_DILATIONS[g])
            o, lse = _attention_group(qkvs[g], bias, g, rows=256)
            os.append(o)
            lses.append(lse)
        yh = _hgrn(hq, hg, hk, hv, og, hgrn_norm[l][None], bsz, seq, rows=256)
        x2 = _merge(os, lses, yh, gates, x2, w_branch_attn[l].astype(BF16), w_branch_hgrn[l].astype(BF16),
                    w_out[l].astype(BF16), post_mix_norm[l][None], seq, tm=512)
        x2 = _ffn(x2, pre_ffn_norm[l][None], w_up[l].astype(BF16), conv_w[l], conv_b[l][None],
                  w_down[l].astype(BF16), post_ffn_norm[l][None], seq, tm=512)
    return x2.reshape(bsz, seq, D_MODEL)
```

```python
import functools
import math

import numpy as np
import jax
import jax.numpy as jnp
from jax import lax
from jax.experimental import pallas as pl
from jax.experimental.pallas import tpu as pltpu

F32 = jnp.float32
BF16 = jnp.bfloat16

D_MODEL = 1024
ATTN_WINDOWS = (128, 512, 2048)
ATTN_DILATIONS = (1, 4, 16)
N_GROUPS = 3
HEAD_DIM = 64
HEADS_PER_GROUP = 8
ATTN_OUT = HEADS_PER_GROUP * HEAD_DIM
GROUP_QKV = 3 * ATTN_OUT
ATTN_QKV = N_GROUPS * GROUP_QKV
ATTN_BLOCK = 128
NEG_INF = -1e30
NUM_BUCKETS = 32
MAX_EXACT = 16
MAX_DISTANCE = 2048
HGRN_HEADS = 4
HGRN_DK = 128
HGRN_WIDTH = HGRN_HEADS * HGRN_DK
HGRN_CHUNK = 64
GATE_WIDTH = 2 * D_MODEL
IN_WIDTH = ATTN_QKV + 4 * HGRN_WIDTH + GATE_WIDTH
D_FF = 2816
CONV_WIDTH = 3
NORM_EPS = 1e-6

LANES = 128
SUBLANES = 8
VMEM_LIMIT = 56 * 1024 * 1024

NT_DIMS = (((1,), (1,)), ((), ()))
TN_DIMS = (((0,), (0,)), ((), ()))


def _rms(xf, w):
    return xf * lax.rsqrt(jnp.mean(xf * xf, axis=-1, keepdims=True) + NORM_EPS) * w


def _resident(shape):
    nd = len(shape)
    return pl.BlockSpec(shape, lambda *_: (0,) * nd, pipeline_mode=pl.Buffered(1))


IN_SUB = 256


def _residue_major_perm(rows, d):
    t = np.arange(rows)
    p = np.zeros((rows, rows), np.float32)
    p[(t % d) * (rows // d) + t // d, t] = 1.0
    return p


def _in_proj_kernel(x_ref, nw_ref, w_ref, lb_ref, pm_ref,
                    qkv0_ref, qkv1_ref, qkv2_ref, hq_ref, hg_ref, hk_ref, hv_ref, og_ref, gate_ref, *, tm):
    sub = IN_SUB
    for t in range(tm // sub):
        rows = slice(t * sub, (t + 1) * sub)
        h = _rms(x_ref[rows, :], nw_ref[...]).astype(BF16)

        def proj(lhs, c0, width):
            return jnp.dot(lhs, w_ref[:, c0:c0 + width], preferred_element_type=F32)

        for g, out_ref in enumerate((qkv0_ref, qkv1_ref, qkv2_ref)):
            d = ATTN_DILATIONS[g]
            h_g = h if d == 1 else jnp.dot(pm_ref[g - 1], h, preferred_element_type=F32).astype(BF16)
            per = sub // d
            for j in range(3):
                cols = slice(j * ATTN_OUT, (j + 1) * ATTN_OUT)
                acc = proj(h_g, (3 * g + j) * ATTN_OUT, ATTN_OUT)
                if j == 0:
                    acc = acc * (HEAD_DIM ** -0.5)
                acc = acc.astype(BF16)
                for r in range(d):
                    out_ref[r, t * per:(t + 1) * per, cols] = acc[r * per:(r + 1) * per, :]

        base = ATTN_QKV
        hq_ref[rows, :] = jax.nn.silu(proj(h, base, HGRN_WIDTH))
        lb = lb_ref[...]
        f = lb + (1.0 - lb) * jax.nn.sigmoid(proj(h, base + HGRN_WIDTH, HGRN_WIDTH))
        hg_ref[rows, :] = jnp.log(f)
        hk_ref[rows, :] = 1.0 - f
        hv_ref[rows, :] = proj(h, base + 2 * HGRN_WIDTH, HGRN_WIDTH).astype(BF16)
        og_ref[rows, :] = jax.nn.silu(proj(h, base + 3 * HGRN_WIDTH, HGRN_WIDTH))

        base = ATTN_QKV + 4 * HGRN_WIDTH
        for c in range(GATE_WIDTH // 512):
            gate_ref[rows, c * 512:(c + 1) * 512] = jax.nn.sigmoid(proj(h, base + c * 512, 512)).astype(BF16)


def _in_proj(x2, nw, w_bf, lb, bsz, seq, tm):
    n = x2.shape[0]
    tps = seq // tm
    row = lambda width: pl.BlockSpec((tm, width), lambda i: (i, 0))
    f32_out = jax.ShapeDtypeStruct((n, HGRN_WIDTH), F32)
    bf16_out = jax.ShapeDtypeStruct((n, HGRN_WIDTH), BF16)
    qkv_specs = [pl.BlockSpec((None, d, tm // d, GROUP_QKV), lambda i: (i // tps, 0, i % tps, 0))
                 for d in ATTN_DILATIONS]
    qkv_shapes = [jax.ShapeDtypeStruct((bsz, d, seq // d, GROUP_QKV), BF16) for d in ATTN_DILATIONS]
    perms = jnp.asarray(np.stack([_residue_major_perm(IN_SUB, d) for d in ATTN_DILATIONS[1:]]), dtype=BF16)
    return pl.pallas_call(
        functools.partial(_in_proj_kernel, tm=tm),
        grid=(n // tm,),
        in_specs=[row(D_MODEL), _resident((1, D_MODEL)), _resident((D_MODEL, IN_WIDTH)),
                  _resident((1, HGRN_WIDTH)), _resident((N_GROUPS - 1, IN_SUB, IN_SUB))],
        out_specs=qkv_specs + [row(HGRN_WIDTH)] * 5 + [row(GATE_WIDTH)],
        out_shape=qkv_shapes + [f32_out, f32_out, f32_out, bf16_out, f32_out]
                  + [jax.ShapeDtypeStruct((n, GATE_WIDTH), BF16)],
        compiler_params=pltpu.CompilerParams(dimension_semantics=("arbitrary",),
                                             vmem_limit_bytes=VMEM_LIMIT),
        name="in_proj",
    )(x2, nw, w_bf, lb, perms)


def _bucket_table(dilation):
    blk = ATTN_BLOCK
    rel = np.arange(blk)[:, None] + blk - np.arange(2 * blk)[None, :]
    dist = np.maximum(rel * dilation, 0)
    nf = np.maximum(dist, 1).astype(np.float32)
    large = MAX_EXACT + (np.log(nf / np.float32(MAX_EXACT)) / np.float32(math.log(MAX_DISTANCE / MAX_EXACT))
                         * np.float32(NUM_BUCKETS - MAX_EXACT)).astype(np.int32)
    large = np.minimum(large, NUM_BUCKETS - 1)
    bucket = np.where(dist < MAX_EXACT, dist, large)
    in_win = (rel >= 0) & (rel <= blk)
    first = in_win & (np.arange(2 * blk)[None, :] >= blk)
    return np.stack([np.where(in_win, bucket, -1), np.where(first, bucket, -1)]).astype(np.int32)


def _bias_kernel(tab_ref, bucket_ref, bias_ref):
    h = pl.program_id(0)
    for v in range(2):
        bucket = bucket_ref[v]
        acc = jnp.full(bucket.shape, NEG_INF, F32)
        for b in range(NUM_BUCKETS):
            acc = jnp.where(bucket == b, tab_ref[b, h], acc)
        bias_ref[v, 0] = acc


def _attn_bias(rel_bias_g, dilation):
    blk = ATTN_BLOCK
    buckets = jnp.asarray(_bucket_table(dilation))
    return pl.pallas_call(
        _bias_kernel,
        grid=(HEADS_PER_GROUP,),
        in_specs=[pl.BlockSpec(memory_space=pltpu.SMEM),
                  pl.BlockSpec((2, blk, 2 * blk), lambda h: (0, 0, 0))],
        out_specs=pl.BlockSpec((2, 1, blk, 2 * blk), lambda h: (0, h, 0, 0)),
        out_shape=jax.ShapeDtypeStruct((2, HEADS_PER_GROUP, blk, 2 * blk), F32),
        compiler_params=pltpu.CompilerParams(dimension_semantics=("arbitrary",)),
        name="attn_bias",
    )(rel_bias_g, buckets)


def _attn_kernel(q_ref, kp_ref, kc_ref, vp_ref, vc_ref, bias_ref, o_ref, lse_ref, kbuf, vbuf, *, rows):
    blk = ATTN_BLOCK
    n = pl.program_id(2)
    kbuf[0:blk] = kp_ref[...]
    kbuf[blk:blk + rows] = kc_ref[...]
    vbuf[0:blk] = vp_ref[...]
    vbuf[blk:blk + rows] = vc_ref[...]

    lane = lax.broadcasted_iota(jnp.int32, (blk, LANES), 1)
    low = lane < HEAD_DIM
    head_mask = (jnp.where(low, 1.0, 0.0).astype(BF16), jnp.where(low, 0.0, 1.0).astype(BF16))

    for i in range(rows // blk):
        variant = jnp.where(n == 0, 1, 0) if i == 0 else 0
        lse_tile = jnp.zeros((blk, LANES), F32)
        for pair in range(HEADS_PER_GROUP // 2):
            cols = slice(pair * LANES, (pair + 1) * LANES)
            q = q_ref[i * blk:(i + 1) * blk, cols]
            kk = kbuf[i * blk:(i + 2) * blk, cols]
            vv = vbuf[i * blk:(i + 2) * blk, cols]
            outs = []
            for hh in range(2):
                head = 2 * pair + hh
                s = lax.dot_general(q * head_mask[hh], kk, NT_DIMS, preferred_element_type=F32)
                s = s + bias_ref[variant, head]
                m = jnp.max(s, axis=-1, keepdims=True)
                p = jnp.exp(s - m)
                l = jnp.sum(p, axis=-1, keepdims=True)
                pv = jnp.dot(p.astype(BF16), vv, preferred_element_type=F32)
                outs.append(pv / l)
                lse_tile = jnp.where(lane == head, m + jnp.log(l), lse_tile)
            o_ref[i * blk:(i + 1) * blk, cols] = jnp.where(low, outs[0], outs[1])
        lse_ref[i * blk:(i + 1) * blk, :] = lse_tile


def _attention_group(qkv, bias, g, rows):
    bsz, d, u, _ = qkv.shape
    blk = ATTN_BLOCK
    rows = min(rows, u)
    nb = u // rows

    def cur(j):
        return pl.BlockSpec((None, None, rows, ATTN_OUT), lambda b, r, n: (b, r, n, j))

    def prev(j):
        return pl.BlockSpec((None, None, blk, ATTN_OUT),
                            lambda b, r, n: (b, r, jnp.maximum(n * (rows // blk) - 1, 0), j))

    return pl.pallas_call(
        functools.partial(_attn_kernel, rows=rows),
        grid=(bsz, d, nb),
        in_specs=[cur(0), prev(1), cur(1), prev(2), cur(2),
                  _resident((2, HEADS_PER_GROUP, blk, 2 * blk))],
        out_specs=[pl.BlockSpec((None, None, rows, ATTN_OUT), lambda b, r, n: (b, r, n, 0)),
                   pl.BlockSpec((None, None, rows, LANES), lambda b, r, n: (b, r, n, 0))],
        out_shape=[jax.ShapeDtypeStruct((bsz, d, u, ATTN_OUT), F32),
                   jax.ShapeDtypeStruct((bsz, d, u, LANES), F32)],
        scratch_shapes=[pltpu.VMEM((blk + rows, ATTN_OUT), BF16), pltpu.VMEM((blk + rows, ATTN_OUT), BF16)],
        compiler_params=pltpu.CompilerParams(dimension_semantics=("arbitrary",) * 3,
                                             vmem_limit_bytes=VMEM_LIMIT),
        name=f"attn_g{g}",
    )(qkv, qkv, qkv, qkv, qkv, bias)


def _hgrn_kernel(q_ref, g_ref, k_ref, v_ref, og_ref, nw_ref, o_ref, st_ref, *, rows):
    c_len = HGRN_CHUNK

    @pl.when(pl.program_id(1) == 0)
    def _():
        st_ref[...] = jnp.zeros_like(st_ref)

    r_i = lax.broadcasted_iota(jnp.int32, (c_len, c_len), 0)
    c_i = lax.broadcasted_iota(jnp.int32, (c_len, c_len), 1)
    tril = r_i >= c_i
    ones_tril = jnp.where(tril, 1.0, 0.0).astype(BF16)
    nw = nw_ref[...]

    for c in range(rows // c_len):
        sl = slice(c * c_len, (c + 1) * c_len)
        g = g_ref[sl, :]
        g1 = g.astype(BF16)
        r1 = g - g1.astype(F32)
        g2 = r1.astype(BF16)
        g3 = (r1 - g2.astype(F32)).astype(BF16)
        cum_all = (jnp.dot(ones_tril, g1, preferred_element_type=F32)
                   + jnp.dot(ones_tril, g2, preferred_element_type=F32)
                   + jnp.dot(ones_tril, g3, preferred_element_type=F32))
        for hd in range(HGRN_HEADS):
            cols = slice(hd * HGRN_DK, (hd + 1) * HGRN_DK)
            cum = cum_all[:, cols]
            mid = cum[c_len // 2 - 1:c_len // 2, :]
            last = cum[c_len - 1:c_len, :]
            q = q_ref[sl, cols]
            k = k_ref[sl, cols]
            v = v_ref[sl, cols]
            q_t = (q * jnp.exp(cum - mid)).astype(BF16)
            k_t = (k * jnp.exp(mid - cum)).astype(BF16)
            a = lax.dot_general(q_t, k_t, NT_DIMS, preferred_element_type=F32)
            a = jnp.where(tril, a, 0.0).astype(BF16)
            state_t = st_ref[hd]
            q_d = (q * jnp.exp(cum)).astype(BF16)
            o = (jnp.dot(a, v, preferred_element_type=F32)
                 + lax.dot_general(q_d, state_t.astype(BF16), NT_DIMS, preferred_element_type=F32))
            k_d = (k * jnp.exp(last - cum)).astype(BF16)
            st_ref[hd] = state_t * jnp.exp(last) + lax.dot_general(v, k_d, TN_DIMS, preferred_element_type=F32)
            y = _rms(o, nw) * og_ref[sl, cols]
            o_ref[sl, cols] = y.astype(BF16)


def _hgrn(hq, hg, hk, hv, og, nw, bsz, seq, rows):
    n = bsz * seq
    tps = seq // rows
    spec = pl.BlockSpec((rows, HGRN_WIDTH), lambda b, t: (b * tps + t, 0))
    return pl.pallas_call(
        functools.partial(_hgrn_kernel, rows=rows),
        grid=(bsz, tps),
        in_specs=[spec] * 5 + [_resident((1, HGRN_DK))],
        out_specs=spec,
        out_shape=jax.ShapeDtypeStruct((n, HGRN_WIDTH), BF16),
        scratch_shapes=[pltpu.VMEM((HGRN_HEADS, HGRN_DK, HGRN_DK), F32)],
        compiler_params=pltpu.CompilerParams(dimension_semantics=("arbitrary",) * 2,
                                             vmem_limit_bytes=VMEM_LIMIT),
        name="hgrn2",
    )(hq, hg, hk, hv, og, nw)


def _merge_kernel(o1_ref, o2_ref, o3_ref, l1_ref, l2_ref, l3_ref, yh_ref, gate_ref, x_ref,
                  expand_ref, wa_ref, wh_ref, wo_ref, nw_ref, out_ref, o_tok, l_tok, *, tm):
    slabs = ATTN_OUT // LANES
    os = [o1_ref[0]]
    lses = [l1_ref[0]]
    for g, (o_ref, l_ref) in enumerate(((o2_ref, l2_ref), (o3_ref, l3_ref))):
        d = ATTN_DILATIONS[g + 1]
        for r in range(d):
            tok_rows = pl.ds(r, tm // d, stride=d)
            l_tok[g, tok_rows, :] = l_ref[r]
            for s in range(slabs):
                o_tok[g, s, tok_rows, :] = o_ref[r, :, s * LANES:(s + 1) * LANES]
        os.append(jnp.concatenate([o_tok[g, s] for s in range(slabs)], axis=1))
        lses.append(l_tok[g])
    mx = jnp.maximum(jnp.maximum(lses[0], lses[1]), lses[2])
    es = [jnp.exp(l - mx) for l in lses]
    den = es[0] + es[1] + es[2]
    expand = expand_ref[...]
    y = None
    for e, o in zip(es, os):
        w = e / den
        w_hi = w.astype(BF16)
        w_lo = (w - w_hi.astype(F32)).astype(BF16)
        w_full = (jnp.dot(w_hi, expand, preferred_element_type=F32)
                  + jnp.dot(w_lo, expand, preferred_element_type=F32))
        term = w_full * o
        y = term if y is None else y + term
    pa = jnp.dot(y.astype(BF16), wa_ref[...], preferred_element_type=F32)
    ph = jnp.dot(yh_ref[...], wh_ref[...], preferred_element_type=F32)
    merged = (gate_ref[:, 0:D_MODEL].astype(F32) * pa + gate_ref[:, D_MODEL:GATE_WIDTH].astype(F32) * ph)
    z = jnp.dot(merged.astype(BF16), wo_ref[...], preferred_element_type=F32)
    out_ref[...] = x_ref[...] + _rms(z, nw_ref[...])


def _merge(os, lses, yh, gates, x2, wa, wh, wo, nw, seq, tm):
    n = x2.shape[0]
    tps = seq // tm
    row = lambda width: pl.BlockSpec((tm, width), lambda i: (i, 0))
    grouped = lambda width: [pl.BlockSpec((None, d, tm // d, width), lambda i: (i // tps, 0, i % tps, 0))
                             for d in ATTN_DILATIONS]
    head_of_col = np.arange(ATTN_OUT) // HEAD_DIM
    expand = jnp.asarray(np.arange(LANES)[:, None] == head_of_col[None, :], dtype=BF16)
    return pl.pallas_call(
        functools.partial(_merge_kernel, tm=tm),
        grid=(n // tm,),
        in_specs=grouped(ATTN_OUT) + grouped(LANES) + [row(HGRN_WIDTH), row(GATE_WIDTH), row(D_MODEL),
                  _resident((LANES, ATTN_OUT)), _resident((ATTN_OUT, D_MODEL)),
                  _resident((HGRN_WIDTH, D_MODEL)), _resident((D_MODEL, D_MODEL)), _resident((1, D_MODEL))],
        out_specs=row(D_MODEL),
        out_shape=jax.ShapeDtypeStruct((n, D_MODEL), F32),
        scratch_shapes=[pltpu.VMEM((N_GROUPS - 1, ATTN_OUT // LANES, tm, LANES), F32),
                        pltpu.VMEM((N_GROUPS - 1, tm, LANES), F32)],
        compiler_params=pltpu.CompilerParams(dimension_semantics=("arbitrary",),
                                             vmem_limit_bytes=VMEM_LIMIT),
        name="merge_out",
    )(*os, *lses, yh, gates, x2, expand, wa, wh, wo, nw)


FFN_CHUNK = 256
FFN_SUB = 256
DOWN_PIECE = 256
CONV_ROWS = 64
HALO = SUBLANES
UBUF_SLOTS = 4


def _ffn_kernel(x_ref, nw1_ref, wup_ref, cw_ref, cb_ref, wdn_ref, nw2_ref, out_ref,
                ubuf, carry, act, *, tm, tiles_per_seq):
    sub = FFN_SUB
    n_sub = tm // sub
    n_chunks = D_FF // FFN_CHUNK
    n_pieces = D_MODEL // DOWN_PIECE
    first = (pl.program_id(0) % tiles_per_seq) == 0

    @pl.when(pl.program_id(0) == 0)
    def _():
        carry[...] = jnp.zeros_like(carry)

    slabs = FFN_CHUNK // LANES

    def up_chunk(s, h, c):
        for half in range(2):
            col0 = half * D_FF + c * FFN_CHUNK
            u = jnp.dot(h, wup_ref[:, col0:col0 + FFN_CHUNK], preferred_element_type=F32)
            for sl in range(slabs):
                slot = (2 * c + half) * slabs + sl
                buf = ubuf.at[c % UBUF_SLOTS, half, sl]
                prev = carry[slot]
                buf[0:HALO, :] = jnp.where(first, 0.0, prev) if s == 0 else prev
                buf[HALO:HALO + sub, :] = u[:, sl * LANES:(sl + 1) * LANES]
                carry[slot] = buf[sub:HALO + sub, :]
        for sl in range(slabs):
            for rb in range(sub // CONV_ROWS):
                conv = []
                for half in range(2):
                    col0 = half * D_FF + c * FFN_CHUNK + sl * LANES
                    buf = ubuf.at[c % UBUF_SLOTS, half, sl]
                    w = cw_ref[:, col0:col0 + LANES]
                    r0 = HALO + rb * CONV_ROWS
                    conv.append(cb_ref[:, col0:col0 + LANES]
                                + w[0:1] * buf[r0 - 2:r0 - 2 + CONV_ROWS, :]
                                + w[1:2] * buf[r0 - 1:r0 - 1 + CONV_ROWS, :]
                                + w[2:3] * buf[r0:r0 + CONV_ROWS, :])
                gate, val = conv
                gelu = 0.5 * gate * (1.0 + lax.erf(gate * (2.0 ** -0.5)))
                act[s, rb * CONV_ROWS:(rb + 1) * CONV_ROWS,
                    c * FFN_CHUNK + sl * LANES:c * FFN_CHUNK + (sl + 1) * LANES] = (gelu * val).astype(BF16)

    piece_after_chunk = {(n_chunks * (p + 1)) // n_pieces - 1: p for p in range(n_pieces)}
    for s in range(n_sub + 1):
        if s < n_sub:
            h = _rms(x_ref[s * sub:(s + 1) * sub, :], nw1_ref[...]).astype(BF16)
        pieces = []
        for c in range(n_chunks):
            if s < n_sub:
                up_chunk(s, h, c)
            if s > 0 and c in piece_after_chunk:
                p = piece_after_chunk[c]
                pieces.append(jnp.dot(act[s - 1], wdn_ref[:, p * DOWN_PIECE:(p + 1) * DOWN_PIECE],
                                      preferred_element_type=F32))
        if s > 0:
            rows = slice((s - 1) * sub, s * sub)
            y = jnp.concatenate(pieces, axis=1)
            out_ref[rows, :] = x_ref[rows, :] + _rms(y, nw2_ref[...])


def _ffn(x2, nw1, wup, cw, cb, wdn, nw2, seq, tm):
    n = x2.shape[0]
    row = pl.BlockSpec((tm, D_MODEL), lambda i: (i, 0))
    return pl.pallas_call(
        functools.partial(_ffn_kernel, tm=tm, tiles_per_seq=seq // tm),
        grid=(n // tm,),
        in_specs=[row, _resident((1, D_MODEL)), _resident((D_MODEL, 2 * D_FF)),
                  _resident((CONV_WIDTH, 2 * D_FF)), _resident((1, 2 * D_FF)),
                  _resident((D_FF, D_MODEL)), _resident((1, D_MODEL))],
        out_specs=row,
        out_shape=jax.ShapeDtypeStruct((n, D_MODEL), F32),
        scratch_shapes=[pltpu.VMEM((UBUF_SLOTS, 2, FFN_CHUNK // LANES, HALO + FFN_SUB, LANES), F32),
                        pltpu.VMEM((2 * D_FF // LANES, HALO, LANES), F32),
                        pltpu.VMEM((tm // FFN_SUB, FFN_SUB, D_FF), BF16)],
        compiler_params=pltpu.CompilerParams(dimension_semantics=("arbitrary",),
                                             vmem_limit_bytes=VMEM_LIMIT),
        name="conv_ffn",
    )(x2, nw1, wup, cw, cb, wdn, nw2)


def kernel(x, pre_mix_norm, w_in, rel_bias, hgrn_lb_raw, hgrn_norm, w_branch_attn, w_branch_hgrn,
           w_out, post_mix_norm, pre_ffn_norm, w_up, conv_w, conv_b, w_down, post_ffn_norm):
    bsz, seq, _ = x.shape
    depth = w_in.shape[0]
    n = bsz * seq
    lbs = jnp.cumsum(jax.nn.softmax(hgrn_lb_raw.astype(F32), axis=0), axis=0)
    x2 = x.reshape(n, D_MODEL)
    for l in range(depth):
        *qkvs, hq, hg, hk, hv, og, gates = _in_proj(
            x2, pre_mix_norm[l][None], w_in[l].astype(BF16), lbs[l][None], bsz, seq, tm=512)
        os, lses = [], []
        for g in range(N_GROUPS):
            bias = _attn_bias(rel_bias[:, g * HEADS_PER_GROUP:(g + 1) * HEADS_PER_GROUP], ATTN_DILATIONS[g])
            o, lse = _attention_group(qkvs[g], bias, g, rows=256)
            os.append(o)
            lses.append(lse)
        yh = _hgrn(hq, hg, hk, hv, og, hgrn_norm[l][None], bsz, seq, rows=256)
        x2 = _merge(os, lses, yh, gates, x2, w_branch_attn[l].astype(BF16), w_branch_hgrn[l].astype(BF16),
                    w_out[l].astype(BF16), post_mix_norm[l][None], seq, tm=512)
        x2 = _ffn(x2, pre_ffn_norm[l][None], w_up[l].astype(BF16), conv_w[l], conv_b[l][None],
                  w_down[l].astype(BF16), post_ffn_norm[l][None], seq, tm=512)
    return x2.reshape(bsz, seq, D_MODEL)
```

```python
import functools
import math

import numpy as np
import jax
import jax.numpy as jnp
from jax import lax
from jax.experimental import pallas as pl
from jax.experimental.pallas import tpu as pltpu

F32 = jnp.float32
BF16 = jnp.bfloat16

D_MODEL = 1024
ATTN_WINDOWS = (128, 512, 2048)
ATTN_DILATIONS = (1, 4, 16)
N_GROUPS = 3
HEAD_DIM = 64
HEADS_PER_GROUP = 8
ATTN_OUT = HEADS_PER_GROUP * HEAD_DIM
GROUP_QKV = 3 * ATTN_OUT
ATTN_QKV = N_GROUPS * GROUP_QKV
ATTN_BLOCK = 128
NEG_INF = -1e30
NUM_BUCKETS = 32
MAX_EXACT = 16
MAX_DISTANCE = 2048
HGRN_HEADS = 4
HGRN_DK = 128
HGRN_WIDTH = HGRN_HEADS * HGRN_DK
HGRN_CHUNK = 64
GATE_WIDTH = 2 * D_MODEL
IN_WIDTH = ATTN_QKV + 4 * HGRN_WIDTH + GATE_WIDTH
D_FF = 2816
CONV_WIDTH = 3
NORM_EPS = 1e-6
LOG2E = math.log2(math.e)

LANES = 128
SUBLANES = 8
VMEM_LIMIT = 56 * 1024 * 1024

NT_DIMS = (((1,), (1,)), ((), ()))
TN_DIMS = (((0,), (0,)), ((), ()))


def _rms(xf, w):
    return xf * lax.rsqrt(jnp.mean(xf * xf, axis=-1, keepdims=True) + NORM_EPS) * w


def _resident(shape):
    nd = len(shape)
    return pl.BlockSpec(shape, lambda *_: (0,) * nd, pipeline_mode=pl.Buffered(1))


IN_SUB = 256


def _residue_major_perm(rows, d):
    t = np.arange(rows)
    p = np.zeros((rows, rows), np.float32)
    p[(t % d) * (rows // d) + t // d, t] = 1.0
    return p


def _in_proj_kernel(x_ref, nw_ref, w_ref, lb_ref, pm_ref,
                    qkv0_ref, qkv1_ref, qkv2_ref, hq_ref, hg_ref, hk_ref, hv_ref, og_ref, gate_ref, *, tm):
    sub = IN_SUB
    for t in range(tm // sub):
        rows = slice(t * sub, (t + 1) * sub)
        h = _rms(x_ref[rows, :], nw_ref[...]).astype(BF16)

        def proj(lhs, c0, width):
            return jnp.dot(lhs, w_ref[:, c0:c0 + width], preferred_element_type=F32)

        for g, out_ref in enumerate((qkv0_ref, qkv1_ref, qkv2_ref)):
            d = ATTN_DILATIONS[g]
            h_g = h if d == 1 else jnp.dot(pm_ref[g - 1], h, preferred_element_type=F32).astype(BF16)
            per = sub // d
            for j in range(3):
                cols = slice(j * ATTN_OUT, (j + 1) * ATTN_OUT)
                acc = proj(h_g, (3 * g + j) * ATTN_OUT, ATTN_OUT)
                if j == 0:
                    acc = acc * (HEAD_DIM ** -0.5 * LOG2E)
                acc = acc.astype(BF16)
                for r in range(d):
                    out_ref[r, t * per:(t + 1) * per, cols] = acc[r * per:(r + 1) * per, :]

        base = ATTN_QKV
        hq_ref[rows, :] = jax.nn.silu(proj(h, base, HGRN_WIDTH))
        lb = lb_ref[...]
        f = lb + (1.0 - lb) * jax.nn.sigmoid(proj(h, base + HGRN_WIDTH, HGRN_WIDTH))
        hg_ref[rows, :] = jnp.log(f)
        hk_ref[rows, :] = 1.0 - f
        hv_ref[rows, :] = proj(h, base + 2 * HGRN_WIDTH, HGRN_WIDTH).astype(BF16)
        og_ref[rows, :] = jax.nn.silu(proj(h, base + 3 * HGRN_WIDTH, HGRN_WIDTH))

        base = ATTN_QKV + 4 * HGRN_WIDTH
        for c in range(GATE_WIDTH // 512):
            gate_ref[rows, c * 512:(c + 1) * 512] = jax.nn.sigmoid(proj(h, base + c * 512, 512)).astype(BF16)


def _in_proj(x2, nw, w_bf, lb, bsz, seq, tm):
    n = x2.shape[0]
    tps = seq // tm
    row = lambda width: pl.BlockSpec((tm, width), lambda i: (i, 0))
    f32_out = jax.ShapeDtypeStruct((n, HGRN_WIDTH), F32)
    bf16_out = jax.ShapeDtypeStruct((n, HGRN_WIDTH), BF16)
    qkv_specs = [pl.BlockSpec((None, d, tm // d, GROUP_QKV), lambda i: (i // tps, 0, i % tps, 0))
                 for d in ATTN_DILATIONS]
    qkv_shapes = [jax.ShapeDtypeStruct((bsz, d, seq // d, GROUP_QKV), BF16) for d in ATTN_DILATIONS]
    perms = jnp.asarray(np.stack([_residue_major_perm(IN_SUB, d) for d in ATTN_DILATIONS[1:]]), dtype=BF16)
    return pl.pallas_call(
        functools.partial(_in_proj_kernel, tm=tm),
        grid=(n // tm,),
        in_specs=[row(D_MODEL), _resident((1, D_MODEL)), _resident((D_MODEL, IN_WIDTH)),
                  _resident((1, HGRN_WIDTH)), _resident((N_GROUPS - 1, IN_SUB, IN_SUB))],
        out_specs=qkv_specs + [row(HGRN_WIDTH)] * 5 + [row(GATE_WIDTH)],
        out_shape=qkv_shapes + [f32_out, f32_out, f32_out, bf16_out, f32_out]
                  + [jax.ShapeDtypeStruct((n, GATE_WIDTH), BF16)],
        compiler_params=pltpu.CompilerParams(dimension_semantics=("arbitrary",),
                                             vmem_limit_bytes=VMEM_LIMIT),
        name="in_proj",
    )(x2, nw, w_bf, lb, perms)


def _bucket_table(dilation):
    blk = ATTN_BLOCK
    rel = np.arange(blk)[:, None] + blk - np.arange(2 * blk)[None, :]
    dist = np.maximum(rel * dilation, 0)
    nf = np.maximum(dist, 1).astype(np.float32)
    large = MAX_EXACT + (np.log(nf / np.float32(MAX_EXACT)) / np.float32(math.log(MAX_DISTANCE / MAX_EXACT))
                         * np.float32(NUM_BUCKETS - MAX_EXACT)).astype(np.int32)
    large = np.minimum(large, NUM_BUCKETS - 1)
    bucket = np.where(dist < MAX_EXACT, dist, large)
    in_win = (rel >= 0) & (rel <= blk)
    first = in_win & (np.arange(2 * blk)[None, :] >= blk)
    return np.stack([np.where(in_win, bucket, -1), np.where(first, bucket, -1)]).astype(np.int32)


def _bias_kernel(tab_ref, bucket_ref, bias_ref):
    h = pl.program_id(0)
    for v in range(2):
        bucket = bucket_ref[v]
        acc = jnp.full(bucket.shape, NEG_INF, F32)
        for b in range(NUM_BUCKETS):
            acc = jnp.where(bucket == b, tab_ref[b, h] * LOG2E, acc)
        bias_ref[v, 0] = acc


def _attn_bias(rel_bias_g, dilation):
    blk = ATTN_BLOCK
    buckets = jnp.asarray(_bucket_table(dilation))
    return pl.pallas_call(
        _bias_kernel,
        grid=(HEADS_PER_GROUP,),
        in_specs=[pl.BlockSpec(memory_space=pltpu.SMEM),
                  pl.BlockSpec((2, blk, 2 * blk), lambda h: (0, 0, 0))],
        out_specs=pl.BlockSpec((2, 1, blk, 2 * blk), lambda h: (0, h, 0, 0)),
        out_shape=jax.ShapeDtypeStruct((2, HEADS_PER_GROUP, blk, 2 * blk), F32),
        compiler_params=pltpu.CompilerParams(dimension_semantics=("arbitrary",)),
        name="attn_bias",
    )(rel_bias_g, buckets)


ATTN_ROWS_PER_STEP = 1024


def _attn_kernel(q_ref, kp_ref, kc_ref, vp_ref, vc_ref, bias_ref, o_ref, m_ref, l_ref, kbuf, vbuf, *, nres, rows):
    blk = ATTN_BLOCK
    pairs = HEADS_PER_GROUP // 2
    n = pl.program_id(2)
    ones = jnp.ones((blk + rows, LANES), BF16)
    for r in range(nres):
        kbuf[r, 0:blk] = kp_ref[r]
        kbuf[r, blk:blk + rows] = kc_ref[r]
        for pair in range(pairs):
            cols = slice(pair * LANES, (pair + 1) * LANES)
            vbuf[r, pair, 0:blk, 0:LANES] = vp_ref[r, :, cols]
            vbuf[r, pair, blk:blk + rows, 0:LANES] = vc_ref[r, :, cols]
            vbuf[r, pair, :, LANES:2 * LANES] = ones

    lane = lax.broadcasted_iota(jnp.int32, (blk, LANES), 1)
    low = lane < HEAD_DIM
    head_mask = (jnp.where(low, 1.0, 0.0).astype(BF16), jnp.where(low, 0.0, 1.0).astype(BF16))

    for r in range(nres):
        for i in range(rows // blk):
            variant = jnp.where(n == 0, 1, 0) if i == 0 else 0
            m_tile = jnp.zeros((blk, LANES), F32)
            l_tile = jnp.ones((blk, LANES), F32)
            for pair in range(pairs):
                cols = slice(pair * LANES, (pair + 1) * LANES)
                q = q_ref[r, i * blk:(i + 1) * blk, cols]
                kk = kbuf[r, i * blk:(i + 2) * blk, cols]
                q2 = jnp.concatenate([q * head_mask[0], q * head_mask[1]], axis=0)
                s = lax.dot_general(q2, kk, NT_DIMS, preferred_element_type=F32)
                s = s + jnp.concatenate([bias_ref[variant, 2 * pair], bias_ref[variant, 2 * pair + 1]], axis=0)
                m = jnp.max(s, axis=-1, keepdims=True)
                p = jnp.exp2(s - m)
                pv = jnp.dot(p.astype(BF16), vbuf[r, pair, i * blk:(i + 2) * blk, :],
                             preferred_element_type=F32)
                o_ref[r, i * blk:(i + 1) * blk, cols] = jnp.where(low, pv[0:blk, 0:LANES],
                                                                   pv[blk:2 * blk, 0:LANES])
                for hh in range(2):
                    head = 2 * pair + hh
                    m_tile = jnp.where(lane == head, m[hh * blk:(hh + 1) * blk], m_tile)
                    l_tile = jnp.where(lane == head, pv[hh * blk:(hh + 1) * blk, LANES:2 * LANES], l_tile)
            m_ref[r, i * blk:(i + 1) * blk, :] = m_tile
            l_ref[r, i * blk:(i + 1) * blk, :] = l_tile


def _attention_group(qkv, bias, g):
    bsz, d, u, _ = qkv.shape
    blk = ATTN_BLOCK
    rows = min(ATTN_ROWS_PER_STEP, u)
    nres = min(ATTN_ROWS_PER_STEP // rows, d)
    nb = u // rows

    def cur(j):
        return pl.BlockSpec((None, nres, rows, ATTN_OUT), lambda b, r, n: (b, r, n, j))

    def prev(j):
        return pl.BlockSpec((None, nres, blk, ATTN_OUT),
                            lambda b, r, n: (b, r, jnp.maximum(n * (rows // blk) - 1, 0), j))

    stat = pl.BlockSpec((None, nres, rows, LANES), lambda b, r, n: (b, r, n, 0))
    stat_shape = jax.ShapeDtypeStruct((bsz, d, u, LANES), F32)
    return pl.pallas_call(
        functools.partial(_attn_kernel, nres=nres, rows=rows),
        grid=(bsz, d // nres, nb),
        in_specs=[cur(0), prev(1), cur(1), prev(2), cur(2),
                  _resident((2, HEADS_PER_GROUP, blk, 2 * blk))],
        out_specs=[pl.BlockSpec((None, nres, rows, ATTN_OUT), lambda b, r, n: (b, r, n, 0)), stat, stat],
        out_shape=[jax.ShapeDtypeStruct((bsz, d, u, ATTN_OUT), F32), stat_shape, stat_shape],
        scratch_shapes=[pltpu.VMEM((nres, blk + rows, ATTN_OUT), BF16),
                        pltpu.VMEM((nres, HEADS_PER_GROUP // 2, blk + rows, 2 * LANES), BF16)],
        compiler_params=pltpu.CompilerParams(dimension_semantics=("arbitrary",) * 3,
                                             vmem_limit_bytes=VMEM_LIMIT),
        name=f"attn_g{g}",
    )(qkv, qkv, qkv, qkv, qkv, bias)


def _hgrn_kernel(q_ref, g_ref, k_ref, v_ref, og_ref, nw_ref, o_ref, st_ref, *, rows):
    c_len = HGRN_CHUNK

    @pl.when(pl.program_id(1) == 0)
    def _():
        st_ref[...] = jnp.zeros_like(st_ref)

    r_i = lax.broadcasted_iota(jnp.int32, (c_len, c_len), 0)
    c_i = lax.broadcasted_iota(jnp.int32, (c_len, c_len), 1)
    tril = r_i >= c_i
    ones_tril = jnp.where(tril, 1.0, 0.0).astype(BF16)
    nw = nw_ref[...]

    for c in range(rows // c_len):
        sl = slice(c * c_len, (c + 1) * c_len)
        g = g_ref[sl, :]
        g1 = g.astype(BF16)
        r1 = g - g1.astype(F32)
        g2 = r1.astype(BF16)
        g3 = (r1 - g2.astype(F32)).astype(BF16)
        cum_all = (jnp.dot(ones_tril, g1, preferred_element_type=F32)
                   + jnp.dot(ones_tril, g2, preferred_element_type=F32)
                   + jnp.dot(ones_tril, g3, preferred_element_type=F32))
        for hd in range(HGRN_HEADS):
            cols = slice(hd * HGRN_DK, (hd + 1) * HGRN_DK)
            cum = cum_all[:, cols]
            mid = cum[c_len // 2 - 1:c_len // 2, :]
            last = cum[c_len - 1:c_len, :]
            q = q_ref[sl, cols]
            k = k_ref[sl, cols]
            v = v_ref[sl, cols]
            q_t = (q * jnp.exp(cum - mid)).astype(BF16)
            k_t = (k * jnp.exp(mid - cum)).astype(BF16)
            a = lax.dot_general(q_t, k_t, NT_DIMS, preferred_element_type=F32)
            a = jnp.where(tril, a, 0.0).astype(BF16)
            state_t = st_ref[hd]
            q_d = (q * jnp.exp(cum)).astype(BF16)
            o = (jnp.dot(a, v, preferred_element_type=F32)
                 + lax.dot_general(q_d, state_t.astype(BF16), NT_DIMS, preferred_element_type=F32))
            k_d = (k * jnp.exp(last - cum)).astype(BF16)
            st_ref[hd] = state_t * jnp.exp(last) + lax.dot_general(v, k_d, TN_DIMS, preferred_element_type=F32)
            y = _rms(o, nw) * og_ref[sl, cols]
            o_ref[sl, cols] = y.astype(BF16)


def _hgrn(hq, hg, hk, hv, og, nw, bsz, seq, rows):
    n = bsz * seq
    tps = seq // rows
    spec = pl.BlockSpec((rows, HGRN_WIDTH), lambda b, t: (b * tps + t, 0))
    return pl.pallas_call(
        functools.partial(_hgrn_kernel, rows=rows),
        grid=(bsz, tps),
        in_specs=[spec] * 5 + [_resident((1, HGRN_DK))],
        out_specs=spec,
        out_shape=jax.ShapeDtypeStruct((n, HGRN_WIDTH), BF16),
        scratch_shapes=[pltpu.VMEM((HGRN_HEADS, HGRN_DK, HGRN_DK), F32)],
        compiler_params=pltpu.CompilerParams(dimension_semantics=("arbitrary",) * 2,
                                             vmem_limit_bytes=VMEM_LIMIT),
        name="hgrn2",
    )(hq, hg, hk, hv, og, nw)


def _merge_kernel(o1_ref, o2_ref, o3_ref, m1_ref, m2_ref, m3_ref, l1_ref, l2_ref, l3_ref,
                  yh_ref, gate_ref, x_ref, expand_ref, wa_ref, wh_ref, wo_ref, nw_ref, out_ref,
                  o_tok, m_tok, l_tok, *, tm):
    slabs = ATTN_OUT // LANES
    os = [o1_ref[0]]
    ms = [m1_ref[0]]
    ls = [l1_ref[0]]
    for g, (o_ref, m_ref, l_ref) in enumerate(((o2_ref, m2_ref, l2_ref), (o3_ref, m3_ref, l3_ref))):
        d = ATTN_DILATIONS[g + 1]
        for r in range(d):
            tok_rows = pl.ds(r, tm // d, stride=d)
            m_tok[g, tok_rows, :] = m_ref[r]
            l_tok[g, tok_rows, :] = l_ref[r]
            for s in range(slabs):
                o_tok[g, s, tok_rows, :] = o_ref[r, :, s * LANES:(s + 1) * LANES]
        os.append(jnp.concatenate([o_tok[g, s] for s in range(slabs)], axis=1))
        ms.append(m_tok[g])
        ls.append(l_tok[g])
    mx = jnp.maximum(jnp.maximum(ms[0], ms[1]), ms[2])
    es = [jnp.exp2(m - mx) for m in ms]
    den = ls[0] * es[0] + ls[1] * es[1] + ls[2] * es[2]
    expand = expand_ref[...]
    y = None
    for e, o in zip(es, os):
        w = e / den
        w_hi = w.astype(BF16)
        w_lo = (w - w_hi.astype(F32)).astype(BF16)
        w_full = (jnp.dot(w_hi, expand, preferred_element_type=F32)
                  + jnp.dot(w_lo, expand, preferred_element_type=F32))
        term = w_full * o
        y = term if y is None else y + term
    pa = jnp.dot(y.astype(BF16), wa_ref[...], preferred_element_type=F32)
    ph = jnp.dot(yh_ref[...], wh_ref[...], preferred_element_type=F32)
    merged = (gate_ref[:, 0:D_MODEL].astype(F32) * pa + gate_ref[:, D_MODEL:GATE_WIDTH].astype(F32) * ph)
    z = jnp.dot(merged.astype(BF16), wo_ref[...], preferred_element_type=F32)
    out_ref[...] = x_ref[...] + _rms(z, nw_ref[...])


def _merge(os, ms, ls, yh, gates, x2, wa, wh, wo, nw, seq, tm):
    n = x2.shape[0]
    tps = seq // tm
    row = lambda width: pl.BlockSpec((tm, width), lambda i: (i, 0))
    grouped = lambda width: [pl.BlockSpec((None, d, tm // d, width), lambda i: (i // tps, 0, i % tps, 0))
                             for d in ATTN_DILATIONS]
    head_of_col = np.arange(ATTN_OUT) // HEAD_DIM
    expand = jnp.asarray(np.arange(LANES)[:, None] == head_of_col[None, :], dtype=BF16)
    return pl.pallas_call(
        functools.partial(_merge_kernel, tm=tm),
        grid=(n // tm,),
        in_specs=grouped(ATTN_OUT) + grouped(LANES) + grouped(LANES) + [
                  row(HGRN_WIDTH), row(GATE_WIDTH), row(D_MODEL),
                  _resident((LANES, ATTN_OUT)), _resident((ATTN_OUT, D_MODEL)),
                  _resident((HGRN_WIDTH, D_MODEL)), _resident((D_MODEL, D_MODEL)), _resident((1, D_MODEL))],
        out_specs=row(D_MODEL),
        out_shape=jax.ShapeDtypeStruct((n, D_MODEL), F32),
        scratch_shapes=[pltpu.VMEM((N_GROUPS - 1, ATTN_OUT // LANES, tm, LANES), F32),
                        pltpu.VMEM((N_GROUPS - 1, tm, LANES), F32),
                        pltpu.VMEM((N_GROUPS - 1, tm, LANES), F32)],
        compiler_params=pltpu.CompilerParams(dimension_semantics=("arbitrary",),
                                             vmem_limit_bytes=VMEM_LIMIT),
        name="merge_out",
    )(*os, *ms, *ls, yh, gates, x2, expand, wa, wh, wo, nw)


FFN_CHUNK = 256
FFN_SUB = 256
DOWN_PIECE = 256
CONV_ROWS = 64
HALO = SUBLANES
UBUF_SLOTS = 4


def _ffn_kernel(x_ref, nw1_ref, wup_ref, cw_ref, cb_ref, wdn_ref, nw2_ref, out_ref,
                ubuf, carry, act, *, tm, tiles_per_seq):
    sub = FFN_SUB
    n_sub = tm // sub
    n_chunks = D_FF // FFN_CHUNK
    n_pieces = D_MODEL // DOWN_PIECE
    first = (pl.program_id(0) % tiles_per_seq) == 0

    @pl.when(pl.program_id(0) == 0)
    def _():
        carry[...] = jnp.zeros_like(carry)

    slabs = FFN_CHUNK // LANES

    def up_chunk(s, h, c):
        for half in range(2):
            col0 = half * D_FF + c * FFN_CHUNK
            u = jnp.dot(h, wup_ref[:, col0:col0 + FFN_CHUNK], preferred_element_type=F32)
            for sl in range(slabs):
                slot = (2 * c + half) * slabs + sl
                buf = ubuf.at[c % UBUF_SLOTS, half, sl]
                prev = carry[slot]
                buf[0:HALO, :] = jnp.where(first, 0.0, prev) if s == 0 else prev
                buf[HALO:HALO + sub, :] = u[:, sl * LANES:(sl + 1) * LANES]
                carry[slot] = buf[sub:HALO + sub, :]
        for sl in range(slabs):
            for rb in range(sub // CONV_ROWS):
                conv = []
                for half in range(2):
                    col0 = half * D_FF + c * FFN_CHUNK + sl * LANES
                    buf = ubuf.at[c % UBUF_SLOTS, half, sl]
                    w = cw_ref[:, col0:col0 + LANES]
                    r0 = HALO + rb * CONV_ROWS
                    conv.append(cb_ref[:, col0:col0 + LANES]
                                + w[0:1] * buf[r0 - 2:r0 - 2 + CONV_ROWS, :]
                                + w[1:2] * buf[r0 - 1:r0 - 1 + CONV_ROWS, :]
                                + w[2:3] * buf[r0:r0 + CONV_ROWS, :])
                gate, val = conv
                gelu = 0.5 * gate * (1.0 + lax.erf(gate * (2.0 ** -0.5)))
                act[s, rb * CONV_ROWS:(rb + 1) * CONV_ROWS,
                    c * FFN_CHUNK + sl * LANES:c * FFN_CHUNK + (sl + 1) * LANES] = (gelu * val).astype(BF16)

    piece_after_chunk = {(n_chunks * (p + 1)) // n_pieces - 1: p for p in range(n_pieces)}
    for s in range(n_sub + 1):
        if s < n_sub:
            h = _rms(x_ref[s * sub:(s + 1) * sub, :], nw1_ref[...]).astype(BF16)
        pieces = []
        for c in range(n_chunks):
            if s < n_sub:
                up_chunk(s, h, c)
            if s > 0 and c in piece_after_chunk:
                p = piece_after_chunk[c]
                pieces.append(jnp.dot(act[s - 1], wdn_ref[:, p * DOWN_PIECE:(p + 1) * DOWN_PIECE],
                                      preferred_element_type=F32))
        if s > 0:
            rows = slice((s - 1) * sub, s * sub)
            y = jnp.concatenate(pieces, axis=1)
            out_ref[rows, :] = x_ref[rows, :] + _rms(y, nw2_ref[...])


def _ffn(x2, nw1, wup, cw, cb, wdn, nw2, seq, tm):
    n = x2.shape[0]
    row = pl.BlockSpec((tm, D_MODEL), lambda i: (i, 0))
    return pl.pallas_call(
        functools.partial(_ffn_kernel, tm=tm, tiles_per_seq=seq // tm),
        grid=(n // tm,),
        in_specs=[row, _resident((1, D_MODEL)), _resident((D_MODEL, 2 * D_FF)),
                  _resident((CONV_WIDTH, 2 * D_FF)), _resident((1, 2 * D_FF)),
                  _resident((D_FF, D_MODEL)), _resident((1, D_MODEL))],
        out_specs=row,
        out_shape=jax.ShapeDtypeStruct((n, D_MODEL), F32),
        scratch_shapes=[pltpu.VMEM((UBUF_SLOTS, 2, FFN_CHUNK // LANES, HALO + FFN_SUB, LANES), F32),
                        pltpu.VMEM((2 * D_FF // LANES, HALO, LANES), F32),
                        pltpu.VMEM((tm // FFN_SUB, FFN_SUB, D_FF), BF16)],
        compiler_params=pltpu.CompilerParams(dimension_semantics=("arbitrary",),
                                             vmem_limit_bytes=VMEM_LIMIT),
        name="conv_ffn",
    )(x2, nw1, wup, cw, cb, wdn, nw2)


def kernel(x, pre_mix_norm, w_in, rel_bias, hgrn_lb_raw, hgrn_norm, w_branch_attn, w_branch_hgrn,
           w_out, post_mix_norm, pre_ffn_norm, w_up, conv_w, conv_b, w_down, post_ffn_norm):
    bsz, seq, _ = x.shape
    depth = w_in.shape[0]
    n = bsz * seq
    lbs = jnp.cumsum(jax.nn.softmax(hgrn_lb_raw.astype(F32), axis=0), axis=0)
    x2 = x.reshape(n, D_MODEL)
    for l in range(depth):
        *qkvs, hq, hg, hk, hv, og, gates = _in_proj(
            x2, pre_mix_norm[l][None], w_in[l].astype(BF16), lbs[l][None], bsz, seq, tm=512)
        os, ms, ls = [], [], []
        for g in range(N_GROUPS):
            bias = _attn_bias(rel_bias[:, g * HEADS_PER_GROUP:(g + 1) * HEADS_PER_GROUP], ATTN_DILATIONS[g])
            o, m_stat, l_stat = _attention_group(qkvs[g], bias, g)
            os.append(o)
            ms.append(m_stat)
            ls.append(l_stat)
        yh = _hgrn(hq, hg, hk, hv, og, hgrn_norm[l][None], bsz, seq, rows=1024)
        x2 = _merge(os, ms, ls, yh, gates, x2, w_branch_attn[l].astype(BF16), w_branch_hgrn[l].astype(BF16),
                    w_out[l].astype(BF16), post_mix_norm[l][None], seq, tm=512)
        x2 = _ffn(x2, pre_ffn_norm[l][None], w_up[l].astype(BF16), conv_w[l], conv_b[l][None],
                  w_down[l].astype(BF16), post_ffn_norm[l][None], seq, tm=512)
    return x2.reshape(bsz, seq, D_MODEL)
```

```python
import functools
import math

import numpy as np
import jax
import jax.numpy as jnp
from jax import lax
from jax.experimental import pallas as pl
from jax.experimental.pallas import tpu as pltpu

F32 = jnp.float32
BF16 = jnp.bfloat16

D_MODEL = 1024
ATTN_WINDOWS = (128, 512, 2048)
ATTN_DILATIONS = (1, 4, 16)
N_GROUPS = 3
HEAD_DIM = 64
HEADS_PER_GROUP = 8
ATTN_OUT = HEADS_PER_GROUP * HEAD_DIM
GROUP_QKV = 3 * ATTN_OUT
ATTN_QKV = N_GROUPS * GROUP_QKV
ATTN_BLOCK = 128
NEG_INF = -1e30
NUM_BUCKETS = 32
MAX_EXACT = 16
MAX_DISTANCE = 2048
HGRN_HEADS = 4
HGRN_DK = 128
HGRN_WIDTH = HGRN_HEADS * HGRN_DK
HGRN_CHUNK = 64
GATE_WIDTH = 2 * D_MODEL
IN_WIDTH = ATTN_QKV + 4 * HGRN_WIDTH + GATE_WIDTH
D_FF = 2816
CONV_WIDTH = 3
NORM_EPS = 1e-6
LOG2E = math.log2(math.e)

LANES = 128
SUBLANES = 8
VMEM_LIMIT = 56 * 1024 * 1024

NT_DIMS = (((1,), (1,)), ((), ()))
TN_DIMS = (((0,), (0,)), ((), ()))


def _rms(xf, w):
    return xf * lax.rsqrt(jnp.mean(xf * xf, axis=-1, keepdims=True) + NORM_EPS) * w


def _resident(shape):
    nd = len(shape)
    return pl.BlockSpec(shape, lambda *_: (0,) * nd, pipeline_mode=pl.Buffered(1))


IN_SUB = 256


def _residue_major_perm(rows, d):
    t = np.arange(rows)
    p = np.zeros((rows, rows), np.float32)
    p[(t % d) * (rows // d) + t // d, t] = 1.0
    return p


def _in_proj_kernel(x_ref, nw_ref, w_ref, lb_ref, pm_ref,
                    qkv0_ref, qkv1_ref, qkv2_ref, hq_ref, hg_ref, hk_ref, hv_ref, og_ref, gate_ref, *, tm):
    sub = IN_SUB
    for t in range(tm // sub):
        rows = slice(t * sub, (t + 1) * sub)
        h = _rms(x_ref[rows, :], nw_ref[...]).astype(BF16)

        def proj(lhs, c0, width):
            return jnp.dot(lhs, w_ref[:, c0:c0 + width], preferred_element_type=F32)

        for g, out_ref in enumerate((qkv0_ref, qkv1_ref, qkv2_ref)):
            d = ATTN_DILATIONS[g]
            h_g = h if d == 1 else jnp.dot(pm_ref[g - 1], h, preferred_element_type=F32).astype(BF16)
            per = sub // d
            for j in range(3):
                cols = slice(j * ATTN_OUT, (j + 1) * ATTN_OUT)
                acc = proj(h_g, (3 * g + j) * ATTN_OUT, ATTN_OUT)
                if j == 0:
                    acc = acc * (HEAD_DIM ** -0.5 * LOG2E)
                acc = acc.astype(BF16)
                for r in range(d):
                    out_ref[r, t * per:(t + 1) * per, cols] = acc[r * per:(r + 1) * per, :]

        base = ATTN_QKV
        hq_ref[rows, :] = jax.nn.silu(proj(h, base, HGRN_WIDTH)).astype(BF16)
        lb = lb_ref[...]
        f = lb + (1.0 - lb) * jax.nn.sigmoid(proj(h, base + HGRN_WIDTH, HGRN_WIDTH))
        hg_ref[rows, :] = jnp.log(f)
        hk_ref[rows, :] = (1.0 - f).astype(BF16)
        hv_ref[rows, :] = proj(h, base + 2 * HGRN_WIDTH, HGRN_WIDTH).astype(BF16)
        og_ref[rows, :] = jax.nn.silu(proj(h, base + 3 * HGRN_WIDTH, HGRN_WIDTH)).astype(BF16)

        base = ATTN_QKV + 4 * HGRN_WIDTH
        for c in range(GATE_WIDTH // 512):
            gate_ref[rows, c * 512:(c + 1) * 512] = jax.nn.sigmoid(proj(h, base + c * 512, 512)).astype(BF16)


def _in_proj(x2, nw, w_bf, lb, bsz, seq, tm):
    n = x2.shape[0]
    tps = seq // tm
    row = lambda width: pl.BlockSpec((tm, width), lambda i: (i, 0))
    f32_out = jax.ShapeDtypeStruct((n, HGRN_WIDTH), F32)
    bf16_out = jax.ShapeDtypeStruct((n, HGRN_WIDTH), BF16)
    qkv_specs = [pl.BlockSpec((None, d, tm // d, GROUP_QKV), lambda i: (i // tps, 0, i % tps, 0))
                 for d in ATTN_DILATIONS]
    qkv_shapes = [jax.ShapeDtypeStruct((bsz, d, seq // d, GROUP_QKV), BF16) for d in ATTN_DILATIONS]
    perms = jnp.asarray(np.stack([_residue_major_perm(IN_SUB, d) for d in ATTN_DILATIONS[1:]]), dtype=BF16)
    return pl.pallas_call(
        functools.partial(_in_proj_kernel, tm=tm),
        grid=(n // tm,),
        in_specs=[row(D_MODEL), _resident((1, D_MODEL)), _resident((D_MODEL, IN_WIDTH)),
                  _resident((1, HGRN_WIDTH)), _resident((N_GROUPS - 1, IN_SUB, IN_SUB))],
        out_specs=qkv_specs + [row(HGRN_WIDTH)] * 5 + [row(GATE_WIDTH)],
        out_shape=qkv_shapes + [bf16_out, f32_out, bf16_out, bf16_out, bf16_out]
                  + [jax.ShapeDtypeStruct((n, GATE_WIDTH), BF16)],
        compiler_params=pltpu.CompilerParams(dimension_semantics=("arbitrary",),
                                             vmem_limit_bytes=VMEM_LIMIT),
        name="in_proj",
    )(x2, nw, w_bf, lb, perms)


def _bucket_table(dilation):
    blk = ATTN_BLOCK
    rel = np.arange(blk)[:, None] + blk - np.arange(2 * blk)[None, :]
    dist = np.maximum(rel * dilation, 0)
    nf = np.maximum(dist, 1).astype(np.float32)
    large = MAX_EXACT + (np.log(nf / np.float32(MAX_EXACT)) / np.float32(math.log(MAX_DISTANCE / MAX_EXACT))
                         * np.float32(NUM_BUCKETS - MAX_EXACT)).astype(np.int32)
    large = np.minimum(large, NUM_BUCKETS - 1)
    bucket = np.where(dist < MAX_EXACT, dist, large)
    in_win = (rel >= 0) & (rel <= blk)
    first = in_win & (np.arange(2 * blk)[None, :] >= blk)
    return np.stack([np.where(in_win, bucket, -1), np.where(first, bucket, -1)]).astype(np.int32)


def _bias_kernel(tab_ref, bucket_ref, bias_ref):
    h = pl.program_id(0)
    for v in range(2):
        bucket = bucket_ref[v]
        acc = jnp.full(bucket.shape, NEG_INF, F32)
        for b in range(NUM_BUCKETS):
            acc = jnp.where(bucket == b, tab_ref[b, h] * LOG2E, acc)
        bias_ref[v, 0] = acc


def _attn_bias(rel_bias_g, dilation):
    blk = ATTN_BLOCK
    buckets = jnp.asarray(_bucket_table(dilation))
    return pl.pallas_call(
        _bias_kernel,
        grid=(HEADS_PER_GROUP,),
        in_specs=[pl.BlockSpec(memory_space=pltpu.SMEM),
                  pl.BlockSpec((2, blk, 2 * blk), lambda h: (0, 0, 0))],
        out_specs=pl.BlockSpec((2, 1, blk, 2 * blk), lambda h: (0, h, 0, 0)),
        out_shape=jax.ShapeDtypeStruct((2, HEADS_PER_GROUP, blk, 2 * blk), F32),
        compiler_params=pltpu.CompilerParams(dimension_semantics=("arbitrary",)),
        name="attn_bias",
    )(rel_bias_g, buckets)


ATTN_ROWS_PER_STEP = 1024


STAT_L_LANE = 64


def _attn_kernel(q_ref, kp_ref, kc_ref, vp_ref, vc_ref, bias_ref, o_ref, stat_ref, kbuf, vbuf, *, nres, rows):
    blk = ATTN_BLOCK
    pairs = HEADS_PER_GROUP // 2
    n = pl.program_id(2)
    ones = jnp.ones((blk + rows, LANES), BF16)
    for r in range(nres):
        kbuf[r, 0:blk] = kp_ref[r]
        kbuf[r, blk:blk + rows] = kc_ref[r]
        for pair in range(pairs):
            cols = slice(pair * LANES, (pair + 1) * LANES)
            vbuf[r, pair, 0:blk, 0:LANES] = vp_ref[r, :, cols]
            vbuf[r, pair, blk:blk + rows, 0:LANES] = vc_ref[r, :, cols]
            vbuf[r, pair, :, LANES:2 * LANES] = ones

    lane = lax.broadcasted_iota(jnp.int32, (blk, LANES), 1)
    low = lane < HEAD_DIM
    head_mask = (jnp.where(low, 1.0, 0.0).astype(BF16), jnp.where(low, 0.0, 1.0).astype(BF16))

    for r in range(nres):
        for i in range(rows // blk):
            variant = jnp.where(n == 0, 1, 0) if i == 0 else 0
            stat = jnp.where(lane < STAT_L_LANE, 0.0, 1.0)
            for pair in range(pairs):
                cols = slice(pair * LANES, (pair + 1) * LANES)
                q = q_ref[r, i * blk:(i + 1) * blk, cols]
                kk = kbuf[r, i * blk:(i + 2) * blk, cols]
                q2 = jnp.concatenate([q * head_mask[0], q * head_mask[1]], axis=0)
                s = lax.dot_general(q2, kk, NT_DIMS, preferred_element_type=F32)
                s = s + jnp.concatenate([bias_ref[variant, 2 * pair], bias_ref[variant, 2 * pair + 1]], axis=0)
                m = jnp.max(s, axis=-1, keepdims=True)
                p = jnp.exp2(s - m)
                pv = jnp.dot(p.astype(BF16), vbuf[r, pair, i * blk:(i + 2) * blk, :],
                             preferred_element_type=F32)
                o_ref[r, i * blk:(i + 1) * blk, cols] = jnp.where(
                    low, pv[0:blk, 0:LANES], pv[blk:2 * blk, 0:LANES]).astype(o_ref.dtype)
                for hh in range(2):
                    head = 2 * pair + hh
                    stat = jnp.where(lane == head, m[hh * blk:(hh + 1) * blk], stat)
                    stat = jnp.where(lane == STAT_L_LANE + head,
                                     pv[hh * blk:(hh + 1) * blk, LANES:2 * LANES], stat)
            stat_ref[r, i * blk:(i + 1) * blk, :] = stat


def _attention_group(qkv, bias, g):
    bsz, d, u, _ = qkv.shape
    blk = ATTN_BLOCK
    rows = min(ATTN_ROWS_PER_STEP, u)
    nres = min(ATTN_ROWS_PER_STEP // rows, d)
    nb = u // rows

    def cur(j):
        return pl.BlockSpec((None, nres, rows, ATTN_OUT), lambda b, r, n: (b, r, n, j))

    def prev(j):
        return pl.BlockSpec((None, nres, blk, ATTN_OUT),
                            lambda b, r, n: (b, r, jnp.maximum(n * (rows // blk) - 1, 0), j))

    stat = pl.BlockSpec((None, nres, rows, LANES), lambda b, r, n: (b, r, n, 0))
    stat_shape = jax.ShapeDtypeStruct((bsz, d, u, LANES), F32)
    return pl.pallas_call(
        functools.partial(_attn_kernel, nres=nres, rows=rows),
        grid=(bsz, d // nres, nb),
        in_specs=[cur(0), prev(1), cur(1), prev(2), cur(2),
                  _resident((2, HEADS_PER_GROUP, blk, 2 * blk))],
        out_specs=[pl.BlockSpec((None, nres, rows, ATTN_OUT), lambda b, r, n: (b, r, n, 0)), stat],
        out_shape=[jax.ShapeDtypeStruct((bsz, d, u, ATTN_OUT), BF16), stat_shape],
        scratch_shapes=[pltpu.VMEM((nres, blk + rows, ATTN_OUT), BF16),
                        pltpu.VMEM((nres, HEADS_PER_GROUP // 2, blk + rows, 2 * LANES), BF16)],
        compiler_params=pltpu.CompilerParams(dimension_semantics=("arbitrary",) * 3,
                                             vmem_limit_bytes=VMEM_LIMIT),
        name=f"attn_g{g}",
    )(qkv, qkv, qkv, qkv, qkv, bias)


def _hgrn_kernel(q_ref, g_ref, k_ref, v_ref, og_ref, nw_ref, o_ref, st_ref, *, rows):
    c_len = HGRN_CHUNK

    @pl.when(pl.program_id(1) == 0)
    def _():
        st_ref[...] = jnp.zeros_like(st_ref)

    r_i = lax.broadcasted_iota(jnp.int32, (c_len, c_len), 0)
    c_i = lax.broadcasted_iota(jnp.int32, (c_len, c_len), 1)
    tril = r_i >= c_i
    ones_tril = jnp.where(tril, 1.0, 0.0).astype(BF16)
    nw = nw_ref[...]

    for c in range(rows // c_len):
        sl = slice(c * c_len, (c + 1) * c_len)
        g = g_ref[sl, :]
        g1 = g.astype(BF16)
        r1 = g - g1.astype(F32)
        g2 = r1.astype(BF16)
        g3 = (r1 - g2.astype(F32)).astype(BF16)
        cum_all = (jnp.dot(ones_tril, g1, preferred_element_type=F32)
                   + jnp.dot(ones_tril, g2, preferred_element_type=F32)
                   + jnp.dot(ones_tril, g3, preferred_element_type=F32))
        for hd in range(HGRN_HEADS):
            cols = slice(hd * HGRN_DK, (hd + 1) * HGRN_DK)
            cum = cum_all[:, cols]
            mid = cum[c_len // 2 - 1:c_len // 2, :]
            last = cum[c_len - 1:c_len, :]
            q = q_ref[sl, cols].astype(F32)
            k = k_ref[sl, cols].astype(F32)
            v = v_ref[sl, cols]
            q_t = (q * jnp.exp(cum - mid)).astype(BF16)
            k_t = (k * jnp.exp(mid - cum)).astype(BF16)
            a = lax.dot_general(q_t, k_t, NT_DIMS, preferred_element_type=F32)
            a = jnp.where(tril, a, 0.0).astype(BF16)
            state_t = st_ref[hd]
            q_d = (q * jnp.exp(cum)).astype(BF16)
            o = (jnp.dot(a, v, preferred_element_type=F32)
                 + lax.dot_general(q_d, state_t.astype(BF16), NT_DIMS, preferred_element_type=F32))
            k_d = (k * jnp.exp(last - cum)).astype(BF16)
            st_ref[hd] = state_t * jnp.exp(last) + lax.dot_general(v, k_d, TN_DIMS, preferred_element_type=F32)
            y = _rms(o, nw) * og_ref[sl, cols].astype(F32)
            o_ref[sl, cols] = y.astype(BF16)


def _hgrn(hq, hg, hk, hv, og, nw, bsz, seq, rows):
    n = bsz * seq
    tps = seq // rows
    spec = pl.BlockSpec((rows, HGRN_WIDTH), lambda b, t: (b * tps + t, 0))
    return pl.pallas_call(
        functools.partial(_hgrn_kernel, rows=rows),
        grid=(bsz, tps),
        in_specs=[spec] * 5 + [_resident((1, HGRN_DK))],
        out_specs=spec,
        out_shape=jax.ShapeDtypeStruct((n, HGRN_WIDTH), BF16),
        scratch_shapes=[pltpu.VMEM((HGRN_HEADS, HGRN_DK, HGRN_DK), F32)],
        compiler_params=pltpu.CompilerParams(dimension_semantics=("arbitrary",) * 2,
                                             vmem_limit_bytes=VMEM_LIMIT),
        name="hgrn2",
    )(hq, hg, hk, hv, og, nw)


MERGE_SUB = 256


def _merge_kernel(o1_ref, o2_ref, o3_ref, s1_ref, s2_ref, s3_ref,
                  yh_ref, gate_ref, x_ref, expand_ref, wa_ref, wh_ref, wo_ref, nw_ref, out_ref,
                  o_tok, s_tok, y_buf, merged_buf, *, tm):
    slabs = ATTN_OUT // LANES
    sub = MERGE_SUB
    lane = lax.broadcasted_iota(jnp.int32, (sub, LANES), 1)
    expand = expand_ref[...]
    for t in range(tm // sub):
        rows = slice(t * sub, (t + 1) * sub)
        os = [o1_ref[0, rows, :].astype(F32)]
        stats = [s1_ref[0, rows, :]]
        for g, (o_ref, s_ref) in enumerate(((o2_ref, s2_ref), (o3_ref, s3_ref))):
            d = ATTN_DILATIONS[g + 1]
            per = sub // d
            for r in range(d):
                tok_rows = pl.ds(r, per, stride=d)
                s_tok[t, g, tok_rows, :] = s_ref[r, t * per:(t + 1) * per, :]
                for s in range(slabs):
                    o_tok[t, g, s, tok_rows, :] = (
                        o_ref[r, t * per:(t + 1) * per, s * LANES:(s + 1) * LANES].astype(F32))
            os.append(jnp.concatenate([o_tok[t, g, s] for s in range(slabs)], axis=1))
            stats.append(s_tok[t, g])
        mx = jnp.maximum(jnp.maximum(stats[0], stats[1]), stats[2])
        es = [jnp.exp2(st - mx) for st in stats]
        ls = [pltpu.roll(st, STAT_L_LANE, axis=1) for st in stats]
        den = ls[0] * es[0] + ls[1] * es[1] + ls[2] * es[2]
        y = None
        for e, o in zip(es, os):
            w = jnp.where(lane < HEADS_PER_GROUP, e / den, 0.0)
            w_hi = w.astype(BF16)
            w_lo = (w - w_hi.astype(F32)).astype(BF16)
            w_full = (jnp.dot(w_hi, expand, preferred_element_type=F32)
                      + jnp.dot(w_lo, expand, preferred_element_type=F32))
            term = w_full * o
            y = term if y is None else y + term
        y_buf[rows, :] = y.astype(BF16)
        pa = jnp.dot(y_buf[rows, :], wa_ref[...], preferred_element_type=F32)
        ph = jnp.dot(yh_ref[rows, :], wh_ref[...], preferred_element_type=F32)
        merged_buf[rows, :] = (gate_ref[rows, 0:D_MODEL].astype(F32) * pa
                               + gate_ref[rows, D_MODEL:GATE_WIDTH].astype(F32) * ph).astype(BF16)
        z = jnp.dot(merged_buf[rows, :], wo_ref[...], preferred_element_type=F32)
        out_ref[rows, :] = x_ref[rows, :] + _rms(z, nw_ref[...])


def _merge(os, stats, yh, gates, x2, wa, wh, wo, nw, seq, tm):
    n = x2.shape[0]
    tps = seq // tm
    row = lambda width: pl.BlockSpec((tm, width), lambda i: (i, 0))
    grouped = lambda width: [pl.BlockSpec((None, d, tm // d, width), lambda i: (i // tps, 0, i % tps, 0))
                             for d in ATTN_DILATIONS]
    head_of_col = np.arange(ATTN_OUT) // HEAD_DIM
    expand = jnp.asarray(np.arange(LANES)[:, None] == head_of_col[None, :], dtype=BF16)
    return pl.pallas_call(
        functools.partial(_merge_kernel, tm=tm),
        grid=(n // tm,),
        in_specs=grouped(ATTN_OUT) + grouped(LANES) + [
                  row(HGRN_WIDTH), row(GATE_WIDTH), row(D_MODEL),
                  _resident((LANES, ATTN_OUT)), _resident((ATTN_OUT, D_MODEL)),
                  _resident((HGRN_WIDTH, D_MODEL)), _resident((D_MODEL, D_MODEL)), _resident((1, D_MODEL))],
        out_specs=row(D_MODEL),
        out_shape=jax.ShapeDtypeStruct((n, D_MODEL), F32),
        scratch_shapes=[pltpu.VMEM((tm // MERGE_SUB, N_GROUPS - 1, ATTN_OUT // LANES, MERGE_SUB, LANES), F32),
                        pltpu.VMEM((tm // MERGE_SUB, N_GROUPS - 1, MERGE_SUB, LANES), F32),
                        pltpu.VMEM((tm, ATTN_OUT), BF16),
                        pltpu.VMEM((tm, D_MODEL), BF16)],
        compiler_params=pltpu.CompilerParams(dimension_semantics=("arbitrary",),
                                             vmem_limit_bytes=VMEM_LIMIT),
        name="merge_out",
    )(*os, *stats, yh, gates, x2, expand, wa, wh, wo, nw)


FFN_CHUNK = 256
FFN_SUB = 256
DOWN_PIECE = 256
CONV_ROWS = 64
HALO = SUBLANES
UBUF_SLOTS = 4


def _ffn_kernel(x_ref, nw1_ref, wup_ref, cw_ref, cb_ref, wdn_ref, nw2_ref, out_ref,
                ubuf, carry, act, hbuf, *, tm, tiles_per_seq):
    sub = FFN_SUB
    n_sub = tm // sub
    n_chunks = D_FF // FFN_CHUNK
    n_pieces = D_MODEL // DOWN_PIECE
    first = (pl.program_id(0) % tiles_per_seq) == 0

    @pl.when(pl.program_id(0) == 0)
    def _():
        carry[...] = jnp.zeros_like(carry)

    slabs = FFN_CHUNK // LANES

    def up_chunk(s, c):
        for half in range(2):
            col0 = half * D_FF + c * FFN_CHUNK
            u = jnp.dot(hbuf[s], wup_ref[:, col0:col0 + FFN_CHUNK], preferred_element_type=F32)
            for sl in range(slabs):
                slot = (2 * c + half) * slabs + sl
                buf = ubuf.at[c % UBUF_SLOTS, half, sl]
                prev = carry[slot]
                buf[0:HALO, :] = jnp.where(first, 0.0, prev) if s == 0 else prev
                buf[HALO:HALO + sub, :] = u[:, sl * LANES:(sl + 1) * LANES]
                carry[slot] = buf[sub:HALO + sub, :]
        for sl in range(slabs):
            for rb in range(sub // CONV_ROWS):
                conv = []
                for half in range(2):
                    col0 = half * D_FF + c * FFN_CHUNK + sl * LANES
                    buf = ubuf.at[c % UBUF_SLOTS, half, sl]
                    w = cw_ref[:, col0:col0 + LANES]
                    r0 = HALO + rb * CONV_ROWS
                    conv.append(cb_ref[:, col0:col0 + LANES]
                                + w[0:1] * buf[r0 - 2:r0 - 2 + CONV_ROWS, :]
                                + w[1:2] * buf[r0 - 1:r0 - 1 + CONV_ROWS, :]
                                + w[2:3] * buf[r0:r0 + CONV_ROWS, :])
                gate, val = conv
                gelu = 0.5 * gate * (1.0 + lax.erf(gate * (2.0 ** -0.5)))
                act[s, rb * CONV_ROWS:(rb + 1) * CONV_ROWS,
                    c * FFN_CHUNK + sl * LANES:c * FFN_CHUNK + (sl + 1) * LANES] = (gelu * val).astype(BF16)

    piece_after_chunk = {(n_chunks * (p + 1)) // n_pieces - 1: p for p in range(n_pieces)}
    for s in range(n_sub + 1):
        if s < n_sub:
            hbuf[s] = _rms(x_ref[s * sub:(s + 1) * sub, :], nw1_ref[...]).astype(BF16)
        pieces = []
        for c in range(n_chunks):
            if s < n_sub:
                up_chunk(s, c)
            if s > 0 and c in piece_after_chunk:
                p = piece_after_chunk[c]
                pieces.append(jnp.dot(act[s - 1], wdn_ref[:, p * DOWN_PIECE:(p + 1) * DOWN_PIECE],
                                      preferred_element_type=F32))
        if s > 0:
            rows = slice((s - 1) * sub, s * sub)
            y = jnp.concatenate(pieces, axis=1)
            out_ref[rows, :] = x_ref[rows, :] + _rms(y, nw2_ref[...])


def _ffn(x2, nw1, wup, cw, cb, wdn, nw2, seq, tm):
    n = x2.shape[0]
    row = pl.BlockSpec((tm, D_MODEL), lambda i: (i, 0))
    return pl.pallas_call(
        functools.partial(_ffn_kernel, tm=tm, tiles_per_seq=seq // tm),
        grid=(n // tm,),
        in_specs=[row, _resident((1, D_MODEL)), _resident((D_MODEL, 2 * D_FF)),
                  _resident((CONV_WIDTH, 2 * D_FF)), _resident((1, 2 * D_FF)),
                  _resident((D_FF, D_MODEL)), _resident((1, D_MODEL))],
        out_specs=row,
        out_shape=jax.ShapeDtypeStruct((n, D_MODEL), F32),
        scratch_shapes=[pltpu.VMEM((UBUF_SLOTS, 2, FFN_CHUNK // LANES, HALO + FFN_SUB, LANES), F32),
                        pltpu.VMEM((2 * D_FF // LANES, HALO, LANES), F32),
                        pltpu.VMEM((tm // FFN_SUB, FFN_SUB, D_FF), BF16),
                        pltpu.VMEM((tm // FFN_SUB, FFN_SUB, D_MODEL), BF16)],
        compiler_params=pltpu.CompilerParams(dimension_semantics=("arbitrary",),
                                             vmem_limit_bytes=VMEM_LIMIT),
        name="conv_ffn",
    )(x2, nw1, wup, cw, cb, wdn, nw2)


def kernel(x, pre_mix_norm, w_in, rel_bias, hgrn_lb_raw, hgrn_norm, w_branch_attn, w_branch_hgrn,
           w_out, post_mix_norm, pre_ffn_norm, w_up, conv_w, conv_b, w_down, post_ffn_norm):
    bsz, seq, _ = x.shape
    depth = w_in.shape[0]
    n = bsz * seq
    lbs = jnp.cumsum(jax.nn.softmax(hgrn_lb_raw.astype(F32), axis=0), axis=0)
    x2 = x.reshape(n, D_MODEL)
    for l in range(depth):
        *qkvs, hq, hg, hk, hv, og, gates = _in_proj(
            x2, pre_mix_norm[l][None], w_in[l].astype(BF16), lbs[l][None], bsz, seq, tm=512)
        os, stats = [], []
        for g in range(N_GROUPS):
            bias = _attn_bias(rel_bias[:, g * HEADS_PER_GROUP:(g + 1) * HEADS_PER_GROUP], ATTN_DILATIONS[g])
            o, stat = _attention_group(qkvs[g], bias, g)
            os.append(o)
            stats.append(stat)
        yh = _hgrn(hq, hg, hk, hv, og, hgrn_norm[l][None], bsz, seq, rows=1024)
        x2 = _merge(os, stats, yh, gates, x2, w_branch_attn[l].astype(BF16), w_branch_hgrn[l].astype(BF16),
                    w_out[l].astype(BF16), post_mix_norm[l][None], seq, tm=512)
        x2 = _ffn(x2, pre_ffn_norm[l][None], w_up[l].astype(BF16), conv_w[l], conv_b[l][None],
                  w_down[l].astype(BF16), post_ffn_norm[l][None], seq, tm=512)
    return x2.reshape(bsz, seq, D_MODEL)
```

```python
import functools
import math

import numpy as np
import jax
import jax.numpy as jnp
from jax import lax
from jax.experimental import pallas as pl
from jax.experimental.pallas import tpu as pltpu

F32 = jnp.float32
BF16 = jnp.bfloat16

D_MODEL = 1024
ATTN_WINDOWS = (128, 512, 2048)
ATTN_DILATIONS = (1, 4, 16)
N_GROUPS = 3
HEAD_DIM = 64
HEADS_PER_GROUP = 8
ATTN_OUT = HEADS_PER_GROUP * HEAD_DIM
GROUP_QKV = 3 * ATTN_OUT
ATTN_QKV = N_GROUPS * GROUP_QKV
ATTN_BLOCK = 128
NEG_INF = -1e30
NUM_BUCKETS = 32
MAX_EXACT = 16
MAX_DISTANCE = 2048
HGRN_HEADS = 4
HGRN_DK = 128
HGRN_WIDTH = HGRN_HEADS * HGRN_DK
HGRN_CHUNK = 128
HGRN_SUB = 64
GATE_WIDTH = 2 * D_MODEL
IN_WIDTH = ATTN_QKV + 4 * HGRN_WIDTH + GATE_WIDTH
D_FF = 2816
CONV_WIDTH = 3
NORM_EPS = 1e-6
LOG2E = math.log2(math.e)

LANES = 128
SUBLANES = 8
VMEM_LIMIT = 56 * 1024 * 1024

NT_DIMS = (((1,), (1,)), ((), ()))
TN_DIMS = (((0,), (0,)), ((), ()))


def _rms(xf, w):
    return xf * lax.rsqrt(jnp.mean(xf * xf, axis=-1, keepdims=True) + NORM_EPS) * w


def _resident(shape):
    nd = len(shape)
    return pl.BlockSpec(shape, lambda *_: (0,) * nd, pipeline_mode=pl.Buffered(1))


IN_SUB = 256


def _residue_major_perm(rows, d):
    t = np.arange(rows)
    p = np.zeros((rows, rows), np.float32)
    p[(t % d) * (rows // d) + t // d, t] = 1.0
    return p


def _in_proj_kernel(x_ref, nw_ref, w_ref, lb_ref, pm_ref,
                    qkv0_ref, qkv1_ref, qkv2_ref, hq_ref, hg_ref, hk_ref, hv_ref, og_ref, gate_ref, *, tm):
    sub = IN_SUB
    for t in range(tm // sub):
        rows = slice(t * sub, (t + 1) * sub)
        h = _rms(x_ref[rows, :], nw_ref[...]).astype(BF16)

        def proj(lhs, c0, width):
            return jnp.dot(lhs, w_ref[:, c0:c0 + width], preferred_element_type=F32)

        for g, out_ref in enumerate((qkv0_ref, qkv1_ref, qkv2_ref)):
            d = ATTN_DILATIONS[g]
            h_g = h if d == 1 else jnp.dot(pm_ref[g - 1], h, preferred_element_type=F32).astype(BF16)
            per = sub // d
            for j in range(3):
                cols = slice(j * ATTN_OUT, (j + 1) * ATTN_OUT)
                acc = proj(h_g, (3 * g + j) * ATTN_OUT, ATTN_OUT)
                if j == 0:
                    acc = acc * (HEAD_DIM ** -0.5 * LOG2E)
                acc = acc.astype(BF16)
                for r in range(d):
                    out_ref[r, t * per:(t + 1) * per, cols] = acc[r * per:(r + 1) * per, :]

        base = ATTN_QKV
        hq_ref[rows, :] = jax.nn.silu(proj(h, base, HGRN_WIDTH)).astype(BF16)
        lb = lb_ref[...]
        f = lb + (1.0 - lb) * jax.nn.sigmoid(proj(h, base + HGRN_WIDTH, HGRN_WIDTH))
        hg_ref[rows, :] = jnp.log(f)
        hk_ref[rows, :] = (1.0 - f).astype(BF16)
        hv_ref[rows, :] = proj(h, base + 2 * HGRN_WIDTH, HGRN_WIDTH).astype(BF16)
        og_ref[rows, :] = jax.nn.silu(proj(h, base + 3 * HGRN_WIDTH, HGRN_WIDTH)).astype(BF16)

        base = ATTN_QKV + 4 * HGRN_WIDTH
        for c in range(GATE_WIDTH // 512):
            gate_ref[rows, c * 512:(c + 1) * 512] = jax.nn.sigmoid(proj(h, base + c * 512, 512)).astype(BF16)


def _in_proj(x2, nw, w_bf, lb, bsz, seq, tm):
    n = x2.shape[0]
    tps = seq // tm
    row = lambda width: pl.BlockSpec((tm, width), lambda i: (i, 0))
    f32_out = jax.ShapeDtypeStruct((n, HGRN_WIDTH), F32)
    bf16_out = jax.ShapeDtypeStruct((n, HGRN_WIDTH), BF16)
    qkv_specs = [pl.BlockSpec((None, d, tm // d, GROUP_QKV), lambda i: (i // tps, 0, i % tps, 0))
                 for d in ATTN_DILATIONS]
    qkv_shapes = [jax.ShapeDtypeStruct((bsz, d, seq // d, GROUP_QKV), BF16) for d in ATTN_DILATIONS]
    perms = jnp.asarray(np.stack([_residue_major_perm(IN_SUB, d) for d in ATTN_DILATIONS[1:]]), dtype=BF16)
    return pl.pallas_call(
        functools.partial(_in_proj_kernel, tm=tm),
        grid=(n // tm,),
        in_specs=[row(D_MODEL), _resident((1, D_MODEL)), _resident((D_MODEL, IN_WIDTH)),
                  _resident((1, HGRN_WIDTH)), _resident((N_GROUPS - 1, IN_SUB, IN_SUB))],
        out_specs=qkv_specs + [row(HGRN_WIDTH)] * 5 + [row(GATE_WIDTH)],
        out_shape=qkv_shapes + [bf16_out, f32_out, bf16_out, bf16_out, bf16_out]
                  + [jax.ShapeDtypeStruct((n, GATE_WIDTH), BF16)],
        compiler_params=pltpu.CompilerParams(dimension_semantics=("arbitrary",),
                                             vmem_limit_bytes=VMEM_LIMIT),
        name="in_proj",
    )(x2, nw, w_bf, lb, perms)


def _bucket_table(dilation):
    blk = ATTN_BLOCK
    rel = np.arange(blk)[:, None] + blk - np.arange(2 * blk)[None, :]
    dist = np.maximum(rel * dilation, 0)
    nf = np.maximum(dist, 1).astype(np.float32)
    large = MAX_EXACT + (np.log(nf / np.float32(MAX_EXACT)) / np.float32(math.log(MAX_DISTANCE / MAX_EXACT))
                         * np.float32(NUM_BUCKETS - MAX_EXACT)).astype(np.int32)
    large = np.minimum(large, NUM_BUCKETS - 1)
    bucket = np.where(dist < MAX_EXACT, dist, large)
    in_win = (rel >= 0) & (rel <= blk)
    first = in_win & (np.arange(2 * blk)[None, :] >= blk)
    return np.stack([np.where(in_win, bucket, -1), np.where(first, bucket, -1)]).astype(np.int32)


def _bias_kernel(tab_ref, bucket_ref, bias_ref):
    h = pl.program_id(0)
    for v in range(2):
        bucket = bucket_ref[0, v]
        acc = jnp.full(bucket.shape, NEG_INF, F32)
        for b in range(NUM_BUCKETS):
            acc = jnp.where(bucket == b, tab_ref[b, h] * LOG2E, acc)
        bias_ref[v, 0] = acc


def _attn_bias(rel_bias):
    blk = ATTN_BLOCK
    n_heads = N_GROUPS * HEADS_PER_GROUP
    buckets = jnp.asarray(np.stack([_bucket_table(d) for d in ATTN_DILATIONS]))
    return pl.pallas_call(
        _bias_kernel,
        grid=(n_heads,),
        in_specs=[pl.BlockSpec(memory_space=pltpu.SMEM),
                  pl.BlockSpec((1, 2, blk, 2 * blk), lambda h: (h // HEADS_PER_GROUP, 0, 0, 0))],
        out_specs=pl.BlockSpec((2, 1, blk, 2 * blk), lambda h: (0, h, 0, 0)),
        out_shape=jax.ShapeDtypeStruct((2, n_heads, blk, 2 * blk), F32),
        compiler_params=pltpu.CompilerParams(dimension_semantics=("arbitrary",)),
        name="attn_bias",
    )(rel_bias, buckets)


ATTN_ROWS_PER_STEP = 2048


STAT_L_LANE = 64


def _attn_kernel(q_ref, kp_ref, kc_ref, vp_ref, vc_ref, bias_ref, o_ref, stat_ref, kbuf, vbuf, *, nres, rows):
    blk = ATTN_BLOCK
    pairs = HEADS_PER_GROUP // 2
    n = pl.program_id(2)
    ones = jnp.ones((blk + rows, LANES), BF16)
    for r in range(nres):
        kbuf[r, 0:blk] = kp_ref[r]
        kbuf[r, blk:blk + rows] = kc_ref[r]
        for pair in range(pairs):
            cols = slice(pair * LANES, (pair + 1) * LANES)
            vbuf[r, pair, 0:blk, 0:LANES] = vp_ref[r, :, cols]
            vbuf[r, pair, blk:blk + rows, 0:LANES] = vc_ref[r, :, cols]
            vbuf[r, pair, :, LANES:2 * LANES] = ones

    lane = lax.broadcasted_iota(jnp.int32, (blk, LANES), 1)
    low = lane < HEAD_DIM
    head_mask = (jnp.where(low, 1.0, 0.0).astype(BF16), jnp.where(low, 0.0, 1.0).astype(BF16))

    for r in range(nres):
        for i in range(rows // blk):
            variant = jnp.where(n == 0, 1, 0) if i == 0 else 0
            stat = jnp.where(lane < STAT_L_LANE, 0.0, 1.0)
            for pair in range(pairs):
                cols = slice(pair * LANES, (pair + 1) * LANES)
                q = q_ref[r, i * blk:(i + 1) * blk, cols]
                kk = kbuf[r, i * blk:(i + 2) * blk, cols]
                q2 = jnp.concatenate([q * head_mask[0], q * head_mask[1]], axis=0)
                s = lax.dot_general(q2, kk, NT_DIMS, preferred_element_type=F32)
                s = s + jnp.concatenate([bias_ref[variant, 2 * pair], bias_ref[variant, 2 * pair + 1]], axis=0)
                m = jnp.max(s, axis=-1, keepdims=True)
                p = jnp.exp2(s - m)
                pv = jnp.dot(p.astype(BF16), vbuf[r, pair, i * blk:(i + 2) * blk, :],
                             preferred_element_type=F32)
                o_ref[r, i * blk:(i + 1) * blk, cols] = jnp.where(
                    low, pv[0:blk, 0:LANES], pv[blk:2 * blk, 0:LANES]).astype(o_ref.dtype)
                for hh in range(2):
                    head = 2 * pair + hh
                    stat = jnp.where(lane == head, m[hh * blk:(hh + 1) * blk], stat)
                    stat = jnp.where(lane == STAT_L_LANE + head,
                                     pv[hh * blk:(hh + 1) * blk, LANES:2 * LANES], stat)
            stat_ref[r, i * blk:(i + 1) * blk, :] = stat


def _attention_group(qkv, bias, g):
    bsz, d, u, _ = qkv.shape
    blk = ATTN_BLOCK
    rows = min(ATTN_ROWS_PER_STEP, u)
    nres = min(ATTN_ROWS_PER_STEP // rows, d)
    nb = u // rows

    def cur(j):
        return pl.BlockSpec((None, nres, rows, ATTN_OUT), lambda b, r, n: (b, r, n, j))

    def prev(j):
        return pl.BlockSpec((None, nres, blk, ATTN_OUT),
                            lambda b, r, n: (b, r, jnp.maximum(n * (rows // blk) - 1, 0), j))

    stat = pl.BlockSpec((None, nres, rows, LANES), lambda b, r, n: (b, r, n, 0))
    stat_shape = jax.ShapeDtypeStruct((bsz, d, u, LANES), F32)
    return pl.pallas_call(
        functools.partial(_attn_kernel, nres=nres, rows=rows),
        grid=(bsz, d // nres, nb),
        in_specs=[cur(0), prev(1), cur(1), prev(2), cur(2),
                  pl.BlockSpec((2, HEADS_PER_GROUP, blk, 2 * blk), lambda b, r, n: (0, g, 0, 0),
                               pipeline_mode=pl.Buffered(1))],
        out_specs=[pl.BlockSpec((None, nres, rows, ATTN_OUT), lambda b, r, n: (b, r, n, 0)), stat],
        out_shape=[jax.ShapeDtypeStruct((bsz, d, u, ATTN_OUT), BF16), stat_shape],
        scratch_shapes=[pltpu.VMEM((nres, blk + rows, ATTN_OUT), BF16),
                        pltpu.VMEM((nres, HEADS_PER_GROUP // 2, blk + rows, 2 * LANES), BF16)],
        compiler_params=pltpu.CompilerParams(dimension_semantics=("arbitrary",) * 3,
                                             vmem_limit_bytes=VMEM_LIMIT),
        name=f"attn_g{g}",
    )(qkv, qkv, qkv, qkv, qkv, bias)


def _hgrn_kernel(q_ref, g_ref, k_ref, v_ref, og_ref, nw_ref, o_ref, st_ref, *, rows):
    c_len, h_len = HGRN_CHUNK, HGRN_SUB

    @pl.when(pl.program_id(1) == 0)
    def _():
        st_ref[...] = jnp.zeros_like(st_ref)

    r_i = lax.broadcasted_iota(jnp.int32, (c_len, c_len), 0)
    c_i = lax.broadcasted_iota(jnp.int32, (c_len, c_len), 1)
    tril = r_i >= c_i
    ones_tril = jnp.where(tril, 1.0, 0.0).astype(BF16)
    same_sub = jnp.logical_and(tril, c_i >= (r_i // h_len) * h_len)
    first_sub = lax.broadcasted_iota(jnp.int32, (c_len, HGRN_DK), 0) < h_len
    no_keys = jnp.zeros((h_len, HGRN_DK), BF16)
    nw = nw_ref[...]

    for c in range(rows // c_len):
        sl = slice(c * c_len, (c + 1) * c_len)
        g = g_ref[sl, :]
        g1 = g.astype(BF16)
        r1 = g - g1.astype(F32)
        g2 = r1.astype(BF16)
        g3 = (r1 - g2.astype(F32)).astype(BF16)
        cum_all = (jnp.dot(ones_tril, g1, preferred_element_type=F32)
                   + jnp.dot(ones_tril, g2, preferred_element_type=F32)
                   + jnp.dot(ones_tril, g3, preferred_element_type=F32))
        for hd in range(HGRN_HEADS):
            cols = slice(hd * HGRN_DK, (hd + 1) * HGRN_DK)
            cum = cum_all[:, cols]
            mid0 = cum[h_len // 2 - 1:h_len // 2, :]
            mid1 = cum[h_len + h_len // 2 - 1:h_len + h_len // 2, :]
            edge = cum[h_len - 1:h_len, :]
            last = cum[c_len - 1:c_len, :]
            q = q_ref[sl, cols].astype(F32)
            k = k_ref[sl, cols].astype(F32)
            v = v_ref[sl, cols]
            rel = cum - jnp.where(first_sub, mid0, mid1)
            q_m = (q * jnp.exp(rel)).astype(BF16)
            k_m = (k * jnp.exp(-rel)).astype(BF16)
            a = jnp.where(same_sub, lax.dot_general(q_m, k_m, NT_DIMS, preferred_element_type=F32), 0.0)
            q_x = (q[h_len:] * jnp.exp(cum[h_len:] - edge)).astype(BF16)
            k_x = jnp.concatenate([(k[:h_len] * jnp.exp(edge - cum[:h_len])).astype(BF16), no_keys], axis=0)
            cross = lax.dot_general(q_x, k_x, NT_DIMS, preferred_element_type=F32)
            a = jnp.concatenate([a[:h_len], a[h_len:] + cross], axis=0).astype(BF16)
            state_t = st_ref[hd]
            q_d = (q * jnp.exp(cum)).astype(BF16)
            o = (jnp.dot(a, v, preferred_element_type=F32)
                 + lax.dot_general(q_d, state_t.astype(BF16), NT_DIMS, preferred_element_type=F32))
            k_d = (k * jnp.exp(last - cum)).astype(BF16)
            st_ref[hd] = state_t * jnp.exp(last) + lax.dot_general(v, k_d, TN_DIMS, preferred_element_type=F32)
            y = _rms(o, nw) * og_ref[sl, cols].astype(F32)
            o_ref[sl, cols] = y.astype(BF16)


def _hgrn(hq, hg, hk, hv, og, nw, bsz, seq, rows):
    n = bsz * seq
    tps = seq // rows
    spec = pl.BlockSpec((rows, HGRN_WIDTH), lambda b, t: (b * tps + t, 0))
    return pl.pallas_call(
        functools.partial(_hgrn_kernel, rows=rows),
        grid=(bsz, tps),
        in_specs=[spec] * 5 + [_resident((1, HGRN_DK))],
        out_specs=spec,
        out_shape=jax.ShapeDtypeStruct((n, HGRN_WIDTH), BF16),
        scratch_shapes=[pltpu.VMEM((HGRN_HEADS, HGRN_DK, HGRN_DK), F32)],
        compiler_params=pltpu.CompilerParams(dimension_semantics=("arbitrary",) * 2,
                                             vmem_limit_bytes=VMEM_LIMIT),
        name="hgrn2",
    )(hq, hg, hk, hv, og, nw)


MERGE_SUB = 256


def _merge_kernel(o1_ref, o2_ref, o3_ref, s1_ref, s2_ref, s3_ref,
                  yh_ref, gate_ref, x_ref, expand_ref, wa_ref, wh_ref, wo_ref, nw_ref, out_ref,
                  o_tok, s_tok, y_buf, merged_buf, *, tm):
    slabs = ATTN_OUT // LANES
    sub = MERGE_SUB
    lane = lax.broadcasted_iota(jnp.int32, (sub, LANES), 1)
    copy_lanes = (lane % STAT_L_LANE) // HEADS_PER_GROUP == 1
    expand = expand_ref[...]
    for t in range(tm // sub):
        rows = slice(t * sub, (t + 1) * sub)
        os = [o1_ref[0, rows, :].astype(F32)]
        stats = [s1_ref[0, rows, :]]
        for g, (o_ref, s_ref) in enumerate(((o2_ref, s2_ref), (o3_ref, s3_ref))):
            d = ATTN_DILATIONS[g + 1]
            per = sub // d
            for r in range(d):
                tok_rows = pl.ds(r, per, stride=d)
                s_tok[t, g, tok_rows, :] = s_ref[r, t * per:(t + 1) * per, :]
                for s in range(slabs):
                    o_tok[t, g, s, tok_rows, :] = (
                        o_ref[r, t * per:(t + 1) * per, s * LANES:(s + 1) * LANES].astype(F32))
            os.append(jnp.concatenate([o_tok[t, g, s] for s in range(slabs)], axis=1))
            stats.append(s_tok[t, g])
        stats = [jnp.where(copy_lanes, pltpu.roll(st, HEADS_PER_GROUP, axis=1), st) for st in stats]
        mx = jnp.maximum(jnp.maximum(stats[0], stats[1]), stats[2])
        es = [jnp.exp2(st - mx) for st in stats]
        ls = [pltpu.roll(st, STAT_L_LANE, axis=1) for st in stats]
        den = ls[0] * es[0] + ls[1] * es[1] + ls[2] * es[2]
        y = None
        for e, o in zip(es, os):
            w = e / den
            w_hi = w.astype(BF16).astype(F32)
            w_split = jnp.where(lane < HEADS_PER_GROUP, w_hi, w - w_hi)
            w_split = jnp.where(lane < 2 * HEADS_PER_GROUP, w_split, 0.0).astype(BF16)
            term = jnp.dot(w_split, expand, preferred_element_type=F32) * o
            y = term if y is None else y + term
        y_buf[rows, :] = y.astype(BF16)
        pa = jnp.dot(y_buf[rows, :], wa_ref[...], preferred_element_type=F32)
        ph = jnp.dot(yh_ref[rows, :], wh_ref[...], preferred_element_type=F32)
        merged_buf[rows, :] = (gate_ref[rows, 0:D_MODEL].astype(F32) * pa
                               + gate_ref[rows, D_MODEL:GATE_WIDTH].astype(F32) * ph).astype(BF16)
        z = jnp.dot(merged_buf[rows, :], wo_ref[...], preferred_element_type=F32)
        out_ref[rows, :] = x_ref[rows, :] + _rms(z, nw_ref[...])


def _merge(os, stats, yh, gates, x2, wa, wh, wo, nw, seq, tm):
    n = x2.shape[0]
    tps = seq // tm
    row = lambda width: pl.BlockSpec((tm, width), lambda i: (i, 0))
    grouped = lambda width: [pl.BlockSpec((None, d, tm // d, width), lambda i: (i // tps, 0, i % tps, 0))
                             for d in ATTN_DILATIONS]
    head_of_col = np.arange(ATTN_OUT) // HEAD_DIM
    head_of_lane = np.where(np.arange(LANES) < 2 * HEADS_PER_GROUP, np.arange(LANES) % HEADS_PER_GROUP, -1)
    expand = jnp.asarray(head_of_lane[:, None] == head_of_col[None, :], dtype=BF16)
    return pl.pallas_call(
        functools.partial(_merge_kernel, tm=tm),
        grid=(n // tm,),
        in_specs=grouped(ATTN_OUT) + grouped(LANES) + [
                  row(HGRN_WIDTH), row(GATE_WIDTH), row(D_MODEL),
                  _resident((LANES, ATTN_OUT)), _resident((ATTN_OUT, D_MODEL)),
                  _resident((HGRN_WIDTH, D_MODEL)), _resident((D_MODEL, D_MODEL)), _resident((1, D_MODEL))],
        out_specs=row(D_MODEL),
        out_shape=jax.ShapeDtypeStruct((n, D_MODEL), F32),
        scratch_shapes=[pltpu.VMEM((tm // MERGE_SUB, N_GROUPS - 1, ATTN_OUT // LANES, MERGE_SUB, LANES), F32),
                        pltpu.VMEM((tm // MERGE_SUB, N_GROUPS - 1, MERGE_SUB, LANES), F32),
                        pltpu.VMEM((tm, ATTN_OUT), BF16),
                        pltpu.VMEM((tm, D_MODEL), BF16)],
        compiler_params=pltpu.CompilerParams(dimension_semantics=("arbitrary",),
                                             vmem_limit_bytes=VMEM_LIMIT),
        name="merge_out",
    )(*os, *stats, yh, gates, x2, expand, wa, wh, wo, nw)


FFN_CHUNK = 256
FFN_SUB = 256
DOWN_PIECE = 256
CONV_ROWS = 64
HALO = SUBLANES
UBUF_SLOTS = 4


def _ffn_kernel(x_ref, nw1_ref, wup_ref, cw_ref, cb_ref, wdn_ref, nw2_ref, out_ref,
                ubuf, carry, act, hbuf, *, tm, tiles_per_seq):
    sub = FFN_SUB
    n_sub = tm // sub
    n_chunks = D_FF // FFN_CHUNK
    n_pieces = D_MODEL // DOWN_PIECE
    first = (pl.program_id(0) % tiles_per_seq) == 0

    @pl.when(pl.program_id(0) == 0)
    def _():
        carry[...] = jnp.zeros_like(carry)

    slabs = FFN_CHUNK // LANES

    def up_chunk(s, c):
        for half in range(2):
            col0 = half * D_FF + c * FFN_CHUNK
            u = jnp.dot(hbuf[s], wup_ref[:, col0:col0 + FFN_CHUNK], preferred_element_type=F32)
            for sl in range(slabs):
                slot = (2 * c + half) * slabs + sl
                buf = ubuf.at[c % UBUF_SLOTS, half, sl]
                prev = carry[slot]
                buf[0:HALO, :] = jnp.where(first, 0.0, prev) if s == 0 else prev
                buf[HALO:HALO + sub, :] = u[:, sl * LANES:(sl + 1) * LANES]
                carry[slot] = buf[sub:HALO + sub, :]
        for sl in range(slabs):
            for rb in range(sub // CONV_ROWS):
                conv = []
                for half in range(2):
                    col0 = half * D_FF + c * FFN_CHUNK + sl * LANES
                    buf = ubuf.at[c % UBUF_SLOTS, half, sl]
                    w = cw_ref[:, col0:col0 + LANES]
                    r0 = HALO + rb * CONV_ROWS
                    conv.append(cb_ref[:, col0:col0 + LANES]
                                + w[0:1] * buf[r0 - 2:r0 - 2 + CONV_ROWS, :]
                                + w[1:2] * buf[r0 - 1:r0 - 1 + CONV_ROWS, :]
                                + w[2:3] * buf[r0:r0 + CONV_ROWS, :])
                gate, val = conv
                gelu = 0.5 * gate * (1.0 + lax.erf(gate * (2.0 ** -0.5)))
                act[s, rb * CONV_ROWS:(rb + 1) * CONV_ROWS,
                    c * FFN_CHUNK + sl * LANES:c * FFN_CHUNK + (sl + 1) * LANES] = (gelu * val).astype(BF16)

    piece_after_chunk = {(n_chunks * (p + 1)) // n_pieces - 1: p for p in range(n_pieces)}

    for s in range(n_sub + 1):
        if s < n_sub:
            hbuf[s] = _rms(x_ref[s * sub:(s + 1) * sub, :], nw1_ref[...]).astype(BF16)
        pieces = []
        for c in range(n_chunks):
            if s < n_sub:
                up_chunk(s, c)
            if s > 0 and c in piece_after_chunk:
                p = piece_after_chunk[c]
                pieces.append(jnp.dot(act[s - 1], wdn_ref[:, p * DOWN_PIECE:(p + 1) * DOWN_PIECE],
                                      preferred_element_type=F32))
        if s > 0:
            rows = slice((s - 1) * sub, s * sub)
            y = jnp.concatenate(pieces, axis=1)
            out_ref[rows, :] = x_ref[rows, :] + _rms(y, nw2_ref[...])


def _ffn(x2, nw1, wup, cw, cb, wdn, nw2, seq, tm):
    n = x2.shape[0]
    row = pl.BlockSpec((tm, D_MODEL), lambda i: (i, 0))
    return pl.pallas_call(
        functools.partial(_ffn_kernel, tm=tm, tiles_per_seq=seq // tm),
        grid=(n // tm,),
        in_specs=[row, _resident((1, D_MODEL)), _resident((D_MODEL, 2 * D_FF)),
                  _resident((CONV_WIDTH, 2 * D_FF)), _resident((1, 2 * D_FF)),
                  _resident((D_FF, D_MODEL)), _resident((1, D_MODEL))],
        out_specs=row,
        out_shape=jax.ShapeDtypeStruct((n, D_MODEL), F32),
        scratch_shapes=[pltpu.VMEM((UBUF_SLOTS, 2, FFN_CHUNK // LANES, HALO + FFN_SUB, LANES), F32),
                        pltpu.VMEM((2 * D_FF // LANES, HALO, LANES), F32),
                        pltpu.VMEM((tm // FFN_SUB, FFN_SUB, D_FF), BF16),
                        pltpu.VMEM((tm // FFN_SUB, FFN_SUB, D_MODEL), BF16)],
        compiler_params=pltpu.CompilerParams(dimension_semantics=("arbitrary",),
                                             vmem_limit_bytes=VMEM_LIMIT),
        name="conv_ffn",
    )(x2, nw1, wup, cw, cb, wdn, nw2)


def kernel(x, pre_mix_norm, w_in, rel_bias, hgrn_lb_raw, hgrn_norm, w_branch_attn, w_branch_hgrn,
           w_out, post_mix_norm, pre_ffn_norm, w_up, conv_w, conv_b, w_down, post_ffn_norm):
    bsz, seq, _ = x.shape
    depth = w_in.shape[0]
    n = bsz * seq
    lbs = jnp.cumsum(jax.nn.softmax(hgrn_lb_raw.astype(F32), axis=0), axis=0)
    x2 = x.reshape(n, D_MODEL)
    bias = _attn_bias(rel_bias)
    for l in range(depth):
        *qkvs, hq, hg, hk, hv, og, gates = _in_proj(
            x2, pre_mix_norm[l][None], w_in[l].astype(BF16), lbs[l][None], bsz, seq, tm=512)
        os, stats = [], []
        for g in range(N_GROUPS):
            o, stat = _attention_group(qkvs[g], bias, g)
            os.append(o)
            stats.append(stat)
        yh = _hgrn(hq, hg, hk, hv, og, hgrn_norm[l][None], bsz, seq, rows=1024)
        x2 = _merge(os, stats, yh, gates, x2, w_branch_attn[l].astype(BF16), w_branch_hgrn[l].astype(BF16),
                    w_out[l].astype(BF16), post_mix_norm[l][None], seq, tm=512)
        x2 = _ffn(x2, pre_ffn_norm[l][None], w_up[l].astype(BF16), conv_w[l], conv_b[l][None],
                  w_down[l].astype(BF16), post_ffn_norm[l][None], seq, tm=512)
    return x2.reshape(bsz, seq, D_MODEL)
```

```python
import functools
import math

import numpy as np
import jax
import jax.numpy as jnp
from jax import lax
from jax.experimental import pallas as pl
from jax.experimental.pallas import tpu as pltpu

F32 = jnp.float32
BF16 = jnp.bfloat16

D_MODEL = 1024
ATTN_WINDOWS = (128, 512, 2048)
ATTN_DILATIONS = (1, 4, 16)
N_GROUPS = 3
HEAD_DIM = 64
HEADS_PER_GROUP = 8
ATTN_OUT = HEADS_PER_GROUP * HEAD_DIM
GROUP_QKV = 3 * ATTN_OUT
ATTN_QKV = N_GROUPS * GROUP_QKV
ATTN_BLOCK = 128
NEG_INF = -1e30
NUM_BUCKETS = 32
MAX_EXACT = 16
MAX_DISTANCE = 2048
HGRN_HEADS = 4
HGRN_DK = 128
HGRN_WIDTH = HGRN_HEADS * HGRN_DK
HGRN_CHUNK = 128
HGRN_SUB = 64
GATE_WIDTH = 2 * D_MODEL
IN_WIDTH = ATTN_QKV + 4 * HGRN_WIDTH + GATE_WIDTH
D_FF = 2816
CONV_WIDTH = 3
NORM_EPS = 1e-6
LOG2E = math.log2(math.e)

LANES = 128
SUBLANES = 8
VMEM_LIMIT = 56 * 1024 * 1024

NT_DIMS = (((1,), (1,)), ((), ()))
TN_DIMS = (((0,), (0,)), ((), ()))


def _rms(xf, w):
    return xf * lax.rsqrt(jnp.mean(xf * xf, axis=-1, keepdims=True) + NORM_EPS) * w


def _resident(shape):
    nd = len(shape)
    return pl.BlockSpec(shape, lambda *_: (0,) * nd, pipeline_mode=pl.Buffered(1))


IN_SUB = 256


def _residue_major_perm(rows, d):
    t = np.arange(rows)
    p = np.zeros((rows, rows), np.float32)
    p[(t % d) * (rows // d) + t // d, t] = 1.0
    return p


def _in_proj_kernel(x_ref, nw_ref, w_ref, lb_ref, pm_ref,
                    qkv0_ref, qkv1_ref, qkv2_ref, hq_ref, hg_ref, hk_ref, hv_ref, og_ref, gate_ref, *, tm):
    sub = IN_SUB
    for t in range(tm // sub):
        rows = slice(t * sub, (t + 1) * sub)
        h = _rms(x_ref[rows, :], nw_ref[...]).astype(BF16)

        def proj(lhs, c0, width):
            return jnp.dot(lhs, w_ref[:, c0:c0 + width], preferred_element_type=F32)

        for g, out_ref in enumerate((qkv0_ref, qkv1_ref, qkv2_ref)):
            d = ATTN_DILATIONS[g]
            h_g = h if d == 1 else jnp.dot(pm_ref[g - 1], h, preferred_element_type=F32).astype(BF16)
            per = sub // d
            for j in range(3):
                cols = slice(j * ATTN_OUT, (j + 1) * ATTN_OUT)
                acc = proj(h_g, (3 * g + j) * ATTN_OUT, ATTN_OUT)
                if j == 0:
                    acc = acc * (HEAD_DIM ** -0.5 * LOG2E)
                acc = acc.astype(BF16)
                for r in range(d):
                    out_ref[r, t * per:(t + 1) * per, cols] = acc[r * per:(r + 1) * per, :]

        base = ATTN_QKV
        hq_ref[rows, :] = jax.nn.silu(proj(h, base, HGRN_WIDTH)).astype(BF16)
        lb = lb_ref[...]
        f = lb + (1.0 - lb) * jax.nn.sigmoid(proj(h, base + HGRN_WIDTH, HGRN_WIDTH))
        hg_ref[rows, :] = jnp.log(f)
        hk_ref[rows, :] = (1.0 - f).astype(BF16)
        hv_ref[rows, :] = proj(h, base + 2 * HGRN_WIDTH, HGRN_WIDTH).astype(BF16)
        og_ref[rows, :] = jax.nn.silu(proj(h, base + 3 * HGRN_WIDTH, HGRN_WIDTH)).astype(BF16)

        base = ATTN_QKV + 4 * HGRN_WIDTH
        for c in range(GATE_WIDTH // 512):
            gate_ref[rows, c * 512:(c + 1) * 512] = jax.nn.sigmoid(proj(h, base + c * 512, 512)).astype(BF16)


def _in_proj(x2, nw, w_bf, lb, bsz, seq, tm):
    n = x2.shape[0]
    tps = seq // tm
    row = lambda width: pl.BlockSpec((tm, width), lambda i: (i, 0))
    f32_out = jax.ShapeDtypeStruct((n, HGRN_WIDTH), F32)
    bf16_out = jax.ShapeDtypeStruct((n, HGRN_WIDTH), BF16)
    qkv_specs = [pl.BlockSpec((None, d, tm // d, GROUP_QKV), lambda i: (i // tps, 0, i % tps, 0))
                 for d in ATTN_DILATIONS]
    qkv_shapes = [jax.ShapeDtypeStruct((bsz, d, seq // d, GROUP_QKV), BF16) for d in ATTN_DILATIONS]
    perms = jnp.asarray(np.stack([_residue_major_perm(IN_SUB, d) for d in ATTN_DILATIONS[1:]]), dtype=BF16)
    return pl.pallas_call(
        functools.partial(_in_proj_kernel, tm=tm),
        grid=(n // tm,),
        in_specs=[row(D_MODEL), _resident((1, D_MODEL)), _resident((D_MODEL, IN_WIDTH)),
                  _resident((1, HGRN_WIDTH)), _resident((N_GROUPS - 1, IN_SUB, IN_SUB))],
        out_specs=qkv_specs + [row(HGRN_WIDTH)] * 5 + [row(GATE_WIDTH)],
        out_shape=qkv_shapes + [bf16_out, f32_out, bf16_out, bf16_out, bf16_out]
                  + [jax.ShapeDtypeStruct((n, GATE_WIDTH), BF16)],
        compiler_params=pltpu.CompilerParams(dimension_semantics=("arbitrary",),
                                             vmem_limit_bytes=VMEM_LIMIT),
        name="in_proj",
    )(x2, nw, w_bf, lb, perms)


def _bucket_table(dilation):
    blk = ATTN_BLOCK
    rel = np.arange(blk)[:, None] + blk - np.arange(2 * blk)[None, :]
    dist = np.maximum(rel * dilation, 0)
    nf = np.maximum(dist, 1).astype(np.float32)
    large = MAX_EXACT + (np.log(nf / np.float32(MAX_EXACT)) / np.float32(math.log(MAX_DISTANCE / MAX_EXACT))
                         * np.float32(NUM_BUCKETS - MAX_EXACT)).astype(np.int32)
    large = np.minimum(large, NUM_BUCKETS - 1)
    bucket = np.where(dist < MAX_EXACT, dist, large)
    in_win = (rel >= 0) & (rel <= blk)
    first = in_win & (np.arange(2 * blk)[None, :] >= blk)
    return np.stack([np.where(in_win, bucket, -1), np.where(first, bucket, -1)]).astype(np.int32)


def _bias_kernel(tab_ref, bucket_ref, bias_ref):
    h = pl.program_id(0)
    for v in range(2):
        bucket = bucket_ref[0, v]
        acc = jnp.full(bucket.shape, NEG_INF, F32)
        for b in range(NUM_BUCKETS):
            acc = jnp.where(bucket == b, tab_ref[b, h] * LOG2E, acc)
        bias_ref[v, 0] = acc


def _attn_bias(rel_bias):
    blk = ATTN_BLOCK
    n_heads = N_GROUPS * HEADS_PER_GROUP
    buckets = jnp.asarray(np.stack([_bucket_table(d) for d in ATTN_DILATIONS]))
    return pl.pallas_call(
        _bias_kernel,
        grid=(n_heads,),
        in_specs=[pl.BlockSpec(memory_space=pltpu.SMEM),
                  pl.BlockSpec((1, 2, blk, 2 * blk), lambda h: (h // HEADS_PER_GROUP, 0, 0, 0))],
        out_specs=pl.BlockSpec((2, 1, blk, 2 * blk), lambda h: (0, h, 0, 0)),
        out_shape=jax.ShapeDtypeStruct((2, n_heads, blk, 2 * blk), F32),
        compiler_params=pltpu.CompilerParams(dimension_semantics=("arbitrary",)),
        name="attn_bias",
    )(rel_bias, buckets)


ATTN_ROWS_PER_STEP = 2048


STAT_L_LANE = 64


def _attn_kernel(q_ref, kp_ref, kc_ref, vp_ref, vc_ref, bias_ref, o_ref, stat_ref, kbuf, vbuf, *, nres, rows):
    blk = ATTN_BLOCK
    pairs = HEADS_PER_GROUP // 2
    n = pl.program_id(2)
    ones = jnp.ones((blk + rows, LANES), BF16)
    for r in range(nres):
        kbuf[r, 0:blk] = kp_ref[r]
        kbuf[r, blk:blk + rows] = kc_ref[r]
        for pair in range(pairs):
            cols = slice(pair * LANES, (pair + 1) * LANES)
            vbuf[r, pair, 0:blk, 0:LANES] = vp_ref[r, :, cols]
            vbuf[r, pair, blk:blk + rows, 0:LANES] = vc_ref[r, :, cols]
            vbuf[r, pair, :, LANES:2 * LANES] = ones

    lane = lax.broadcasted_iota(jnp.int32, (blk, LANES), 1)
    low = lane < HEAD_DIM
    head_mask = (jnp.where(low, 1.0, 0.0).astype(BF16), jnp.where(low, 0.0, 1.0).astype(BF16))

    for r in range(nres):
        for i in range(rows // blk):
            variant = jnp.where(n == 0, 1, 0) if i == 0 else 0
            stat = jnp.where(lane < STAT_L_LANE, 0.0, 1.0)
            for pair in range(pairs):
                cols = slice(pair * LANES, (pair + 1) * LANES)
                q = q_ref[r, i * blk:(i + 1) * blk, cols]
                kk = kbuf[r, i * blk:(i + 2) * blk, cols]
                q2 = jnp.concatenate([q * head_mask[0], q * head_mask[1]], axis=0)
                s = lax.dot_general(q2, kk, NT_DIMS, preferred_element_type=F32)
                s = s + jnp.concatenate([bias_ref[variant, 2 * pair], bias_ref[variant, 2 * pair + 1]], axis=0)
                m = jnp.max(s, axis=-1, keepdims=True)
                p = jnp.exp2(s - m)
                pv = jnp.dot(p.astype(BF16), vbuf[r, pair, i * blk:(i + 2) * blk, :],
                             preferred_element_type=F32)
                o_ref[r, i * blk:(i + 1) * blk, cols] = jnp.where(
                    low, pv[0:blk, 0:LANES], pv[blk:2 * blk, 0:LANES]).astype(o_ref.dtype)
                for hh in range(2):
                    head = 2 * pair + hh
                    stat = jnp.where(lane == head, m[hh * blk:(hh + 1) * blk], stat)
                    stat = jnp.where(lane == STAT_L_LANE + head,
                                     pv[hh * blk:(hh + 1) * blk, LANES:2 * LANES], stat)
            stat_ref[r, i * blk:(i + 1) * blk, :] = stat


def _attention_group(qkv, bias, g):
    bsz, d, u, _ = qkv.shape
    blk = ATTN_BLOCK
    rows = min(ATTN_ROWS_PER_STEP, u)
    nres = min(ATTN_ROWS_PER_STEP // rows, d)
    nb = u // rows

    def cur(j):
        return pl.BlockSpec((None, nres, rows, ATTN_OUT), lambda b, r, n: (b, r, n, j))

    def prev(j):
        return pl.BlockSpec((None, nres, blk, ATTN_OUT),
                            lambda b, r, n: (b, r, jnp.maximum(n * (rows // blk) - 1, 0), j))

    stat = pl.BlockSpec((None, nres, rows, LANES), lambda b, r, n: (b, r, n, 0))
    stat_shape = jax.ShapeDtypeStruct((bsz, d, u, LANES), F32)
    return pl.pallas_call(
        functools.partial(_attn_kernel, nres=nres, rows=rows),
        grid=(bsz, d // nres, nb),
        in_specs=[cur(0), prev(1), cur(1), prev(2), cur(2),
                  pl.BlockSpec((2, HEADS_PER_GROUP, blk, 2 * blk), lambda b, r, n: (0, g, 0, 0),
                               pipeline_mode=pl.Buffered(1))],
        out_specs=[pl.BlockSpec((None, nres, rows, ATTN_OUT), lambda b, r, n: (b, r, n, 0)), stat],
        out_shape=[jax.ShapeDtypeStruct((bsz, d, u, ATTN_OUT), BF16), stat_shape],
        scratch_shapes=[pltpu.VMEM((nres, blk + rows, ATTN_OUT), BF16),
                        pltpu.VMEM((nres, HEADS_PER_GROUP // 2, blk + rows, 2 * LANES), BF16)],
        compiler_params=pltpu.CompilerParams(dimension_semantics=("arbitrary",) * 3,
                                             vmem_limit_bytes=VMEM_LIMIT),
        name=f"attn_g{g}",
    )(qkv, qkv, qkv, qkv, qkv, bias)


def _hgrn_tile(q_ref, g_ref, k_ref, v_ref, og_ref, nw_ref, o_ref, st_ref, rows):
    c_len, h_len = HGRN_CHUNK, HGRN_SUB
    r_i = lax.broadcasted_iota(jnp.int32, (c_len, c_len), 0)
    c_i = lax.broadcasted_iota(jnp.int32, (c_len, c_len), 1)
    tril = r_i >= c_i
    ones_tril = jnp.where(tril, 1.0, 0.0).astype(BF16)
    same_sub = jnp.logical_and(tril, c_i >= (r_i // h_len) * h_len)
    first_sub = lax.broadcasted_iota(jnp.int32, (c_len, HGRN_DK), 0) < h_len
    no_keys = jnp.zeros((h_len, HGRN_DK), BF16)
    nw = nw_ref[...]

    for c in range(rows // c_len):
        sl = slice(c * c_len, (c + 1) * c_len)
        g = g_ref[sl, :]
        g1 = g.astype(BF16)
        r1 = g - g1.astype(F32)
        g2 = r1.astype(BF16)
        g3 = (r1 - g2.astype(F32)).astype(BF16)
        cum_all = (jnp.dot(ones_tril, g1, preferred_element_type=F32)
                   + jnp.dot(ones_tril, g2, preferred_element_type=F32)
                   + jnp.dot(ones_tril, g3, preferred_element_type=F32))
        for hd in range(HGRN_HEADS):
            cols = slice(hd * HGRN_DK, (hd + 1) * HGRN_DK)
            cum = cum_all[:, cols]
            mid0 = cum[h_len // 2 - 1:h_len // 2, :]
            mid1 = cum[h_len + h_len // 2 - 1:h_len + h_len // 2, :]
            edge = cum[h_len - 1:h_len, :]
            last = cum[c_len - 1:c_len, :]
            q = q_ref[sl, cols].astype(F32)
            k = k_ref[sl, cols].astype(F32)
            v = v_ref[sl, cols]
            rel = cum - jnp.where(first_sub, mid0, mid1)
            q_m = (q * jnp.exp(rel)).astype(BF16)
            k_m = (k * jnp.exp(-rel)).astype(BF16)
            a = jnp.where(same_sub, lax.dot_general(q_m, k_m, NT_DIMS, preferred_element_type=F32), 0.0)
            q_x = (q[h_len:] * jnp.exp(cum[h_len:] - edge)).astype(BF16)
            k_x = jnp.concatenate([(k[:h_len] * jnp.exp(edge - cum[:h_len])).astype(BF16), no_keys], axis=0)
            cross = lax.dot_general(q_x, k_x, NT_DIMS, preferred_element_type=F32)
            a = jnp.concatenate([a[:h_len], a[h_len:] + cross], axis=0).astype(BF16)
            state_t = st_ref[hd]
            q_d = (q * jnp.exp(cum)).astype(BF16)
            o = (jnp.dot(a, v, preferred_element_type=F32)
                 + lax.dot_general(q_d, state_t.astype(BF16), NT_DIMS, preferred_element_type=F32))
            k_d = (k * jnp.exp(last - cum)).astype(BF16)
            st_ref[hd] = state_t * jnp.exp(last) + lax.dot_general(v, k_d, TN_DIMS, preferred_element_type=F32)
            y = _rms(o, nw) * og_ref[sl, cols].astype(F32)
            o_ref[sl, cols] = y.astype(BF16)


MERGE_SUB = 256


def _mix_kernel(o1_ref, o2_ref, o3_ref, s1_ref, s2_ref, s3_ref,
                hq_ref, hg_ref, hk_ref, hv_ref, og_ref, hnw_ref,
                gate_ref, x_ref, expand_ref, wa_ref, wh_ref, wo_ref, nw_ref, out_ref,
                o_tok, s_tok, y_buf, merged_buf, yh_ref, st_ref, *, tm, tiles_per_seq):
    @pl.when(pl.program_id(0) % tiles_per_seq == 0)
    def _():
        st_ref[...] = jnp.zeros_like(st_ref)

    _hgrn_tile(hq_ref, hg_ref, hk_ref, hv_ref, og_ref, hnw_ref, yh_ref, st_ref, tm)

    slabs = ATTN_OUT // LANES
    sub = MERGE_SUB
    lane = lax.broadcasted_iota(jnp.int32, (sub, LANES), 1)
    copy_lanes = (lane % STAT_L_LANE) // HEADS_PER_GROUP == 1
    expand = expand_ref[...]
    for t in range(tm // sub):
        rows = slice(t * sub, (t + 1) * sub)
        os = [o1_ref[0, rows, :].astype(F32)]
        stats = [s1_ref[0, rows, :]]
        for g, (o_ref, s_ref) in enumerate(((o2_ref, s2_ref), (o3_ref, s3_ref))):
            d = ATTN_DILATIONS[g + 1]
            per = sub // d
            for r in range(d):
                tok_rows = pl.ds(r, per, stride=d)
                s_tok[t, g, tok_rows, :] = s_ref[r, t * per:(t + 1) * per, :]
                for s in range(slabs):
                    o_tok[t, g, s, tok_rows, :] = (
                        o_ref[r, t * per:(t + 1) * per, s * LANES:(s + 1) * LANES].astype(F32))
            os.append(jnp.concatenate([o_tok[t, g, s] for s in range(slabs)], axis=1))
            stats.append(s_tok[t, g])
        stats = [jnp.where(copy_lanes, pltpu.roll(st, HEADS_PER_GROUP, axis=1), st) for st in stats]
        mx = jnp.maximum(jnp.maximum(stats[0], stats[1]), stats[2])
        es = [jnp.exp2(st - mx) for st in stats]
        ls = [pltpu.roll(st, STAT_L_LANE, axis=1) for st in stats]
        den = ls[0] * es[0] + ls[1] * es[1] + ls[2] * es[2]
        y = None
        for e, o in zip(es, os):
            w = e / den
            w_hi = w.astype(BF16).astype(F32)
            w_split = jnp.where(lane < HEADS_PER_GROUP, w_hi, w - w_hi)
            w_split = jnp.where(lane < 2 * HEADS_PER_GROUP, w_split, 0.0).astype(BF16)
            term = jnp.dot(w_split, expand, preferred_element_type=F32) * o
            y = term if y is None else y + term
        y_buf[rows, :] = y.astype(BF16)
        pa = jnp.dot(y_buf[rows, :], wa_ref[...], preferred_element_type=F32)
        ph = jnp.dot(yh_ref[rows, :], wh_ref[...], preferred_element_type=F32)
        merged_buf[rows, :] = (gate_ref[rows, 0:D_MODEL].astype(F32) * pa
                               + gate_ref[rows, D_MODEL:GATE_WIDTH].astype(F32) * ph).astype(BF16)
        z = jnp.dot(merged_buf[rows, :], wo_ref[...], preferred_element_type=F32)
        out_ref[rows, :] = x_ref[rows, :] + _rms(z, nw_ref[...])


def _mix(os, stats, hgrn_cols, hnw, gates, x2, wa, wh, wo, nw, seq, tm):
    n = x2.shape[0]
    tps = seq // tm
    row = lambda width: pl.BlockSpec((tm, width), lambda i: (i, 0))
    grouped = lambda width: [pl.BlockSpec((None, d, tm // d, width), lambda i: (i // tps, 0, i % tps, 0))
                             for d in ATTN_DILATIONS]
    head_of_col = np.arange(ATTN_OUT) // HEAD_DIM
    head_of_lane = np.where(np.arange(LANES) < 2 * HEADS_PER_GROUP, np.arange(LANES) % HEADS_PER_GROUP, -1)
    expand = jnp.asarray(head_of_lane[:, None] == head_of_col[None, :], dtype=BF16)
    return pl.pallas_call(
        functools.partial(_mix_kernel, tm=tm, tiles_per_seq=tps),
        grid=(n // tm,),
        in_specs=grouped(ATTN_OUT) + grouped(LANES) + [row(HGRN_WIDTH)] * 5 + [_resident((1, HGRN_DK)),
                  row(GATE_WIDTH), row(D_MODEL),
                  _resident((LANES, ATTN_OUT)), _resident((ATTN_OUT, D_MODEL)),
                  _resident((HGRN_WIDTH, D_MODEL)), _resident((D_MODEL, D_MODEL)), _resident((1, D_MODEL))],
        out_specs=row(D_MODEL),
        out_shape=jax.ShapeDtypeStruct((n, D_MODEL), F32),
        scratch_shapes=[pltpu.VMEM((tm // MERGE_SUB, N_GROUPS - 1, ATTN_OUT // LANES, MERGE_SUB, LANES), F32),
                        pltpu.VMEM((tm // MERGE_SUB, N_GROUPS - 1, MERGE_SUB, LANES), F32),
                        pltpu.VMEM((tm, ATTN_OUT), BF16),
                        pltpu.VMEM((tm, D_MODEL), BF16),
                        pltpu.VMEM((tm, HGRN_WIDTH), BF16),
                        pltpu.VMEM((HGRN_HEADS, HGRN_DK, HGRN_DK), F32)],
        compiler_params=pltpu.CompilerParams(dimension_semantics=("arbitrary",),
                                             vmem_limit_bytes=VMEM_LIMIT),
        name="mix_out",
    )(*os, *stats, *hgrn_cols, hnw, gates, x2, expand, wa, wh, wo, nw)


FFN_CHUNK = 256
FFN_SUB = 256
DOWN_PIECE = 256
CONV_ROWS = 64
HALO = SUBLANES
UBUF_SLOTS = 4


def _ffn_kernel(x_ref, nw1_ref, wup_ref, cw_ref, cb_ref, wdn_ref, nw2_ref, out_ref,
                ubuf, carry, act, hbuf, *, tm, tiles_per_seq):
    sub = FFN_SUB
    n_sub = tm // sub
    n_chunks = D_FF // FFN_CHUNK
    n_pieces = D_MODEL // DOWN_PIECE
    first = (pl.program_id(0) % tiles_per_seq) == 0

    @pl.when(pl.program_id(0) == 0)
    def _():
        carry[...] = jnp.zeros_like(carry)

    slabs = FFN_CHUNK // LANES

    def up_chunk(s, c):
        for half in range(2):
            col0 = half * D_FF + c * FFN_CHUNK
            u = jnp.dot(hbuf[s], wup_ref[:, col0:col0 + FFN_CHUNK], preferred_element_type=F32)
            for sl in range(slabs):
                slot = (2 * c + half) * slabs + sl
                buf = ubuf.at[c % UBUF_SLOTS, half, sl]
                prev = carry[slot]
                buf[0:HALO, :] = jnp.where(first, 0.0, prev) if s == 0 else prev
                buf[HALO:HALO + sub, :] = u[:, sl * LANES:(sl + 1) * LANES]
                carry[slot] = buf[sub:HALO + sub, :]
        for sl in range(slabs):
            for rb in range(sub // CONV_ROWS):
                conv = []
                for half in range(2):
                    col0 = half * D_FF + c * FFN_CHUNK + sl * LANES
                    buf = ubuf.at[c % UBUF_SLOTS, half, sl]
                    w = cw_ref[:, col0:col0 + LANES]
                    r0 = HALO + rb * CONV_ROWS
                    conv.append(cb_ref[:, col0:col0 + LANES]
                                + w[0:1] * buf[r0 - 2:r0 - 2 + CONV_ROWS, :]
                                + w[1:2] * buf[r0 - 1:r0 - 1 + CONV_ROWS, :]
                                + w[2:3] * buf[r0:r0 + CONV_ROWS, :])
                gate, val = conv
                gelu = 0.5 * gate * (1.0 + lax.erf(gate * (2.0 ** -0.5)))
                act[s, rb * CONV_ROWS:(rb + 1) * CONV_ROWS,
                    c * FFN_CHUNK + sl * LANES:c * FFN_CHUNK + (sl + 1) * LANES] = (gelu * val).astype(BF16)

    piece_after_chunk = {(n_chunks * (p + 1)) // n_pieces - 1: p for p in range(n_pieces)}

    for s in range(n_sub + 1):
        if s < n_sub:
            hbuf[s] = _rms(x_ref[s * sub:(s + 1) * sub, :], nw1_ref[...]).astype(BF16)
        pieces = []
        for c in range(n_chunks):
            if s < n_sub:
                up_chunk(s, c)
            if s > 0 and c in piece_after_chunk:
                p = piece_after_chunk[c]
                pieces.append(jnp.dot(act[s - 1], wdn_ref[:, p * DOWN_PIECE:(p + 1) * DOWN_PIECE],
                                      preferred_element_type=F32))
        if s > 0:
            rows = slice((s - 1) * sub, s * sub)
            y = jnp.concatenate(pieces, axis=1)
            out_ref[rows, :] = x_ref[rows, :] + _rms(y, nw2_ref[...])


def _ffn(x2, nw1, wup, cw, cb, wdn, nw2, seq, tm):
    n = x2.shape[0]
    row = pl.BlockSpec((tm, D_MODEL), lambda i: (i, 0))
    return pl.pallas_call(
        functools.partial(_ffn_kernel, tm=tm, tiles_per_seq=seq // tm),
        grid=(n // tm,),
        in_specs=[row, _resident((1, D_MODEL)), _resident((D_MODEL, 2 * D_FF)),
                  _resident((CONV_WIDTH, 2 * D_FF)), _resident((1, 2 * D_FF)),
                  _resident((D_FF, D_MODEL)), _resident((1, D_MODEL))],
        out_specs=row,
        out_shape=jax.ShapeDtypeStruct((n, D_MODEL), F32),
        scratch_shapes=[pltpu.VMEM((UBUF_SLOTS, 2, FFN_CHUNK // LANES, HALO + FFN_SUB, LANES), F32),
                        pltpu.VMEM((2 * D_FF // LANES, HALO, LANES), F32),
                        pltpu.VMEM((tm // FFN_SUB, FFN_SUB, D_FF), BF16),
                        pltpu.VMEM((tm // FFN_SUB, FFN_SUB, D_MODEL), BF16)],
        compiler_params=pltpu.CompilerParams(dimension_semantics=("arbitrary",),
                                             vmem_limit_bytes=VMEM_LIMIT),
        name="conv_ffn",
    )(x2, nw1, wup, cw, cb, wdn, nw2)


def kernel(x, pre_mix_norm, w_in, rel_bias, hgrn_lb_raw, hgrn_norm, w_branch_attn, w_branch_hgrn,
           w_out, post_mix_norm, pre_ffn_norm, w_up, conv_w, conv_b, w_down, post_ffn_norm):
    bsz, seq, _ = x.shape
    depth = w_in.shape[0]
    n = bsz * seq
    lbs = jnp.cumsum(jax.nn.softmax(hgrn_lb_raw.astype(F32), axis=0), axis=0)
    x2 = x.reshape(n, D_MODEL)
    bias = _attn_bias(rel_bias)
    for l in range(depth):
        *qkvs, hq, hg, hk, hv, og, gates = _in_proj(
            x2, pre_mix_norm[l][None], w_in[l].astype(BF16), lbs[l][None], bsz, seq, tm=512)
        os, stats = [], []
        for g in range(N_GROUPS):
            o, stat = _attention_group(qkvs[g], bias, g)
            os.append(o)
            stats.append(stat)
        x2 = _mix(os, stats, (hq, hg, hk, hv, og), hgrn_norm[l][None], gates, x2,
                  w_branch_attn[l].astype(BF16), w_branch_hgrn[l].astype(BF16),
                  w_out[l].astype(BF16), post_mix_norm[l][None], seq, tm=512)
        x2 = _ffn(x2, pre_ffn_norm[l][None], w_up[l].astype(BF16), conv_w[l], conv_b[l][None],
                  w_down[l].astype(BF16), post_ffn_norm[l][None], seq, tm=512)
    return x2.reshape(bsz, seq, D_MODEL)
```

```python
import functools
import math

import numpy as np
import jax
import jax.numpy as jnp
from jax import lax
from jax.experimental import pallas as pl
from jax.experimental.pallas import tpu as pltpu

F32 = jnp.float32
BF16 = jnp.bfloat16

D_MODEL = 1024
ATTN_WINDOWS = (128, 512, 2048)
ATTN_DILATIONS = (1, 4, 16)
N_GROUPS = 3
HEAD_DIM = 64
HEADS_PER_GROUP = 8
ATTN_OUT = HEADS_PER_GROUP * HEAD_DIM
GROUP_QKV = 3 * ATTN_OUT
ATTN_QKV = N_GROUPS * GROUP_QKV
ATTN_BLOCK = 128
NEG_INF = -1e30
NUM_BUCKETS = 32
MAX_EXACT = 16
MAX_DISTANCE = 2048
HGRN_HEADS = 4
HGRN_DK = 128
HGRN_WIDTH = HGRN_HEADS * HGRN_DK
HGRN_CHUNK = 128
HGRN_SUB = 64
GATE_WIDTH = 2 * D_MODEL
IN_WIDTH = ATTN_QKV + 4 * HGRN_WIDTH + GATE_WIDTH
D_FF = 2816
CONV_WIDTH = 3
NORM_EPS = 1e-6
LOG2E = math.log2(math.e)

LANES = 128
SUBLANES = 8
VMEM_LIMIT = 56 * 1024 * 1024

NT_DIMS = (((1,), (1,)), ((), ()))
TN_DIMS = (((0,), (0,)), ((), ()))


def _rms(xf, w):
    return xf * lax.rsqrt(jnp.mean(xf * xf, axis=-1, keepdims=True) + NORM_EPS) * w


def _resident(shape):
    nd = len(shape)
    return pl.BlockSpec(shape, lambda *_: (0,) * nd, pipeline_mode=pl.Buffered(1))


IN_SUB = 256


def _residue_major_perm(rows, d):
    t = np.arange(rows)
    p = np.zeros((rows, rows), np.float32)
    p[(t % d) * (rows // d) + t // d, t] = 1.0
    return p


def _in_proj_kernel(x_ref, nw_ref, w_ref, lb_ref, pm_ref,
                    qkv0_ref, qkv1_ref, qkv2_ref, hq_ref, hg_ref, hk_ref, hv_ref, og_ref, gate_ref, *, tm):
    sub = IN_SUB
    for t in range(tm // sub):
        rows = slice(t * sub, (t + 1) * sub)
        h = _rms(x_ref[rows, :], nw_ref[...]).astype(BF16)

        def proj(lhs, c0, width):
            return jnp.dot(lhs, w_ref[:, c0:c0 + width], preferred_element_type=F32)

        for g, out_ref in enumerate((qkv0_ref, qkv1_ref, qkv2_ref)):
            d = ATTN_DILATIONS[g]
            h_g = h if d == 1 else jnp.dot(pm_ref[g - 1], h, preferred_element_type=F32).astype(BF16)
            per = sub // d
            for j in range(3):
                cols = slice(j * ATTN_OUT, (j + 1) * ATTN_OUT)
                acc = proj(h_g, (3 * g + j) * ATTN_OUT, ATTN_OUT)
                if j == 0:
                    acc = acc * (HEAD_DIM ** -0.5 * LOG2E)
                acc = acc.astype(BF16)
                for r in range(d):
                    out_ref[r, t * per:(t + 1) * per, cols] = acc[r * per:(r + 1) * per, :]

        base = ATTN_QKV
        hq_ref[rows, :] = jax.nn.silu(proj(h, base, HGRN_WIDTH)).astype(BF16)
        lb = lb_ref[...]
        f = lb + (1.0 - lb) * jax.nn.sigmoid(proj(h, base + HGRN_WIDTH, HGRN_WIDTH))
        hg_ref[rows, :] = jnp.log(f)
        hk_ref[rows, :] = (1.0 - f).astype(BF16)
        hv_ref[rows, :] = proj(h, base + 2 * HGRN_WIDTH, HGRN_WIDTH).astype(BF16)
        og_ref[rows, :] = jax.nn.silu(proj(h, base + 3 * HGRN_WIDTH, HGRN_WIDTH)).astype(BF16)

        base = ATTN_QKV + 4 * HGRN_WIDTH
        for c in range(GATE_WIDTH // 512):
            gate_ref[rows, c * 512:(c + 1) * 512] = jax.nn.sigmoid(proj(h, base + c * 512, 512)).astype(BF16)


def _in_proj(x2, nw, w_bf, lb, bsz, seq, tm):
    n = x2.shape[0]
    tps = seq // tm
    row = lambda width: pl.BlockSpec((tm, width), lambda i: (i, 0))
    f32_out = jax.ShapeDtypeStruct((n, HGRN_WIDTH), F32)
    bf16_out = jax.ShapeDtypeStruct((n, HGRN_WIDTH), BF16)
    qkv_specs = [pl.BlockSpec((None, d, tm // d, GROUP_QKV), lambda i: (i // tps, 0, i % tps, 0))
                 for d in ATTN_DILATIONS]
    qkv_shapes = [jax.ShapeDtypeStruct((bsz, d, seq // d, GROUP_QKV), BF16) for d in ATTN_DILATIONS]
    perms = jnp.asarray(np.stack([_residue_major_perm(IN_SUB, d) for d in ATTN_DILATIONS[1:]]), dtype=BF16)
    return pl.pallas_call(
        functools.partial(_in_proj_kernel, tm=tm),
        grid=(n // tm,),
        in_specs=[row(D_MODEL), _resident((1, D_MODEL)), _resident((D_MODEL, IN_WIDTH)),
                  _resident((1, HGRN_WIDTH)), _resident((N_GROUPS - 1, IN_SUB, IN_SUB))],
        out_specs=qkv_specs + [row(HGRN_WIDTH)] * 5 + [row(GATE_WIDTH)],
        out_shape=qkv_shapes + [bf16_out, f32_out, bf16_out, bf16_out, bf16_out]
                  + [jax.ShapeDtypeStruct((n, GATE_WIDTH), BF16)],
        compiler_params=pltpu.CompilerParams(dimension_semantics=("arbitrary",),
                                             vmem_limit_bytes=VMEM_LIMIT),
        name="in_proj",
    )(x2, nw, w_bf, lb, perms)


def _bucket_table(dilation):
    blk = ATTN_BLOCK
    rel = np.arange(blk)[:, None] + blk - np.arange(2 * blk)[None, :]
    dist = np.maximum(rel * dilation, 0)
    nf = np.maximum(dist, 1).astype(np.float32)
    large = MAX_EXACT + (np.log(nf / np.float32(MAX_EXACT)) / np.float32(math.log(MAX_DISTANCE / MAX_EXACT))
                         * np.float32(NUM_BUCKETS - MAX_EXACT)).astype(np.int32)
    large = np.minimum(large, NUM_BUCKETS - 1)
    bucket = np.where(dist < MAX_EXACT, dist, large)
    in_win = (rel >= 0) & (rel <= blk)
    return np.where(in_win, bucket, -1).astype(np.int32)


def _bias_kernel(tab_ref, bucket_ref, bias_ref):
    h = pl.program_id(0)
    bucket = bucket_ref[0]
    acc = jnp.full(bucket.shape, NEG_INF, F32)
    for b in range(NUM_BUCKETS):
        acc = jnp.where(bucket == b, tab_ref[b, h] * LOG2E, acc)
    bias_ref[0, 0] = acc
    col = lax.broadcasted_iota(jnp.int32, bucket.shape, 1)
    bias_ref[1, 0] = jnp.where(col >= ATTN_BLOCK, acc, NEG_INF)


def _attn_bias(rel_bias):
    blk = ATTN_BLOCK
    n_heads = N_GROUPS * HEADS_PER_GROUP
    buckets = jnp.asarray(np.stack([_bucket_table(d) for d in ATTN_DILATIONS]))
    return pl.pallas_call(
        _bias_kernel,
        grid=(n_heads,),
        in_specs=[pl.BlockSpec(memory_space=pltpu.SMEM),
                  pl.BlockSpec((1, blk, 2 * blk), lambda h: (h // HEADS_PER_GROUP, 0, 0))],
        out_specs=pl.BlockSpec((2, 1, blk, 2 * blk), lambda h: (0, h, 0, 0)),
        out_shape=jax.ShapeDtypeStruct((2, n_heads, blk, 2 * blk), F32),
        compiler_params=pltpu.CompilerParams(dimension_semantics=("arbitrary",)),
        name="attn_bias",
    )(rel_bias, buckets)


ATTN_ROWS_PER_STEP = 2048


STAT_L_LANE = 64


def _attn_kernel(q_ref, kp_ref, kc_ref, vp_ref, vc_ref, bias_ref, o_ref, stat_ref, kbuf, vbuf, *, nres, rows):
    blk = ATTN_BLOCK
    pairs = HEADS_PER_GROUP // 2
    n = pl.program_id(2)
    ones = jnp.ones((blk + rows, LANES), BF16)
    for r in range(nres):
        kbuf[r, 0:blk] = kp_ref[r]
        kbuf[r, blk:blk + rows] = kc_ref[r]
        for pair in range(pairs):
            cols = slice(pair * LANES, (pair + 1) * LANES)
            vbuf[r, pair, 0:blk, 0:LANES] = vp_ref[r, :, cols]
            vbuf[r, pair, blk:blk + rows, 0:LANES] = vc_ref[r, :, cols]
            vbuf[r, pair, :, LANES:2 * LANES] = ones

    lane = lax.broadcasted_iota(jnp.int32, (blk, LANES), 1)
    low = lane < HEAD_DIM
    head_mask = (jnp.where(low, 1.0, 0.0).astype(BF16), jnp.where(low, 0.0, 1.0).astype(BF16))

    for r in range(nres):
        for i in range(rows // blk):
            variant = jnp.where(n == 0, 1, 0) if i == 0 else 0
            stat = jnp.where(lane < STAT_L_LANE, 0.0, 1.0)
            for pair in range(pairs):
                cols = slice(pair * LANES, (pair + 1) * LANES)
                q = q_ref[r, i * blk:(i + 1) * blk, cols]
                kk = kbuf[r, i * blk:(i + 2) * blk, cols]
                q2 = jnp.concatenate([q * head_mask[0], q * head_mask[1]], axis=0)
                s = lax.dot_general(q2, kk, NT_DIMS, preferred_element_type=F32)
                s = s + jnp.concatenate([bias_ref[variant, 2 * pair], bias_ref[variant, 2 * pair + 1]], axis=0)
                m = jnp.max(s, axis=-1, keepdims=True)
                p = jnp.exp2(s - m)
                pv = jnp.dot(p.astype(BF16), vbuf[r, pair, i * blk:(i + 2) * blk, :],
                             preferred_element_type=F32)
                o_ref[r, i * blk:(i + 1) * blk, cols] = jnp.where(
                    low, pv[0:blk, 0:LANES], pv[blk:2 * blk, 0:LANES]).astype(o_ref.dtype)
                for hh in range(2):
                    head = 2 * pair + hh
                    stat = jnp.where(lane == head, m[hh * blk:(hh + 1) * blk], stat)
                    stat = jnp.where(lane == STAT_L_LANE + head,
                                     pv[hh * blk:(hh + 1) * blk, LANES:2 * LANES], stat)
            stat_ref[r, i * blk:(i + 1) * blk, :] = stat


def _attention_group(qkv, bias, g):
    bsz, d, u, _ = qkv.shape
    blk = ATTN_BLOCK
    rows = min(ATTN_ROWS_PER_STEP, u)
    nres = min(ATTN_ROWS_PER_STEP // rows, d)
    nb = u // rows

    def cur(j):
        return pl.BlockSpec((None, nres, rows, ATTN_OUT), lambda b, r, n: (b, r, n, j))

    def prev(j):
        return pl.BlockSpec((None, nres, blk, ATTN_OUT),
                            lambda b, r, n: (b, r, jnp.maximum(n * (rows // blk) - 1, 0), j))

    stat = pl.BlockSpec((None, nres, rows, LANES), lambda b, r, n: (b, r, n, 0))
    stat_shape = jax.ShapeDtypeStruct((bsz, d, u, LANES), F32)
    return pl.pallas_call(
        functools.partial(_attn_kernel, nres=nres, rows=rows),
        grid=(bsz, d // nres, nb),
        in_specs=[cur(0), prev(1), cur(1), prev(2), cur(2),
                  pl.BlockSpec((2, HEADS_PER_GROUP, blk, 2 * blk), lambda b, r, n: (0, g, 0, 0),
                               pipeline_mode=pl.Buffered(1))],
        out_specs=[pl.BlockSpec((None, nres, rows, ATTN_OUT), lambda b, r, n: (b, r, n, 0)), stat],
        out_shape=[jax.ShapeDtypeStruct((bsz, d, u, ATTN_OUT), BF16), stat_shape],
        scratch_shapes=[pltpu.VMEM((nres, blk + rows, ATTN_OUT), BF16),
                        pltpu.VMEM((nres, HEADS_PER_GROUP // 2, blk + rows, 2 * LANES), BF16)],
        compiler_params=pltpu.CompilerParams(dimension_semantics=("arbitrary",) * 3,
                                             vmem_limit_bytes=VMEM_LIMIT),
        name=f"attn_g{g}",
    )(qkv, qkv, qkv, qkv, qkv, bias)


def _hgrn_chunks(q_ref, g_ref, k_ref, v_ref, og_ref, nw_ref, o_ref, st_ref):
    c_len, h_len = HGRN_CHUNK, HGRN_SUB
    r_i = lax.broadcasted_iota(jnp.int32, (c_len, c_len), 0)
    c_i = lax.broadcasted_iota(jnp.int32, (c_len, c_len), 1)
    tril = r_i >= c_i
    ones_tril = jnp.where(tril, 1.0, 0.0).astype(BF16)
    same_sub = jnp.logical_and(tril, c_i >= (r_i // h_len) * h_len)
    first_sub = lax.broadcasted_iota(jnp.int32, (c_len, HGRN_DK), 0) < h_len
    no_keys = jnp.zeros((h_len, HGRN_DK), BF16)
    nw = nw_ref[...]

    def chunk(c):
        sl = slice(c * c_len, (c + 1) * c_len)
        g = g_ref[sl, :]
        g1 = g.astype(BF16)
        g2 = (g - g1.astype(F32)).astype(BF16)
        cum_all = (jnp.dot(ones_tril, g1, preferred_element_type=F32)
                   + jnp.dot(ones_tril, g2, preferred_element_type=F32))
        for hd in range(HGRN_HEADS):
            cols = slice(hd * HGRN_DK, (hd + 1) * HGRN_DK)
            cum = cum_all[:, cols]
            mid0 = cum[h_len // 2 - 1:h_len // 2, :]
            mid1 = cum[h_len + h_len // 2 - 1:h_len + h_len // 2, :]
            edge = cum[h_len - 1:h_len, :]
            last = cum[c_len - 1:c_len, :]
            q = q_ref[sl, cols].astype(F32)
            k = k_ref[sl, cols].astype(F32)
            v = v_ref[sl, cols]
            rel = cum - jnp.where(first_sub, mid0, mid1)
            q_m = (q * jnp.exp(rel)).astype(BF16)
            k_m = (k * jnp.exp(-rel)).astype(BF16)
            a = jnp.where(same_sub, lax.dot_general(q_m, k_m, NT_DIMS, preferred_element_type=F32), 0.0)
            q_x = (q[h_len:] * jnp.exp(cum[h_len:] - edge)).astype(BF16)
            k_x = jnp.concatenate([(k[:h_len] * jnp.exp(edge - cum[:h_len])).astype(BF16), no_keys], axis=0)
            cross = lax.dot_general(q_x, k_x, NT_DIMS, preferred_element_type=F32)
            a = jnp.concatenate([a[:h_len], a[h_len:] + cross], axis=0).astype(BF16)
            state_t = st_ref[hd]
            q_d = (q * jnp.exp(cum)).astype(BF16)
            o = (jnp.dot(a, v, preferred_element_type=F32)
                 + lax.dot_general(q_d, state_t.astype(BF16), NT_DIMS, preferred_element_type=F32))
            k_d = (k * jnp.exp(last - cum)).astype(BF16)
            st_ref[hd] = state_t * jnp.exp(last) + lax.dot_general(v, k_d, TN_DIMS, preferred_element_type=F32)
            y = _rms(o, nw) * og_ref[sl, cols].astype(F32)
            o_ref[sl, cols] = y.astype(BF16)

    return chunk


MERGE_SUB = 256


def _mix_kernel(o1_ref, o2_ref, o3_ref, s1_ref, s2_ref, s3_ref,
                hq_ref, hg_ref, hk_ref, hv_ref, og_ref, hnw_ref,
                gate_ref, x_ref, expand_ref, wa_ref, wh_ref, wo_ref, nw_ref, out_ref,
                o_tok, s_tok, y_buf, merged_buf, yh_ref, st_ref, *, tm, tiles_per_seq):
    @pl.when(pl.program_id(0) % tiles_per_seq == 0)
    def _():
        st_ref[...] = jnp.zeros_like(st_ref)

    hgrn_chunk = _hgrn_chunks(hq_ref, hg_ref, hk_ref, hv_ref, og_ref, hnw_ref, yh_ref, st_ref)

    slabs = ATTN_OUT // LANES
    sub = MERGE_SUB
    chunks_per_sub = sub // HGRN_CHUNK
    lane = lax.broadcasted_iota(jnp.int32, (sub, LANES), 1)
    copy_lanes = (lane % STAT_L_LANE) // HEADS_PER_GROUP == 1
    expand = expand_ref[...]

    def merge_weights(t):
        rows = slice(t * sub, (t + 1) * sub)
        os = [o1_ref[0, rows, :].astype(F32)]
        stats = [s1_ref[0, rows, :]]
        for g, (o_ref, s_ref) in enumerate(((o2_ref, s2_ref), (o3_ref, s3_ref))):
            d = ATTN_DILATIONS[g + 1]
            per = sub // d
            for r in range(d):
                tok_rows = pl.ds(r, per, stride=d)
                s_tok[t, g, tok_rows, :] = s_ref[r, t * per:(t + 1) * per, :]
                for s in range(slabs):
                    o_tok[t, g, s, tok_rows, :] = (
                        o_ref[r, t * per:(t + 1) * per, s * LANES:(s + 1) * LANES].astype(F32))
            os.append(jnp.concatenate([o_tok[t, g, s] for s in range(slabs)], axis=1))
            stats.append(s_tok[t, g])
        stats = [jnp.where(copy_lanes, pltpu.roll(st, HEADS_PER_GROUP, axis=1), st) for st in stats]
        mx = jnp.maximum(jnp.maximum(stats[0], stats[1]), stats[2])
        es = [jnp.exp2(st - mx) for st in stats]
        ls = [pltpu.roll(st, STAT_L_LANE, axis=1) for st in stats]
        den = ls[0] * es[0] + ls[1] * es[1] + ls[2] * es[2]
        y = None
        for e, o in zip(es, os):
            w = e / den
            w_hi = w.astype(BF16).astype(F32)
            w_split = jnp.where(lane < HEADS_PER_GROUP, w_hi, w - w_hi)
            w_split = jnp.where(lane < 2 * HEADS_PER_GROUP, w_split, 0.0).astype(BF16)
            term = jnp.dot(w_split, expand, preferred_element_type=F32) * o
            y = term if y is None else y + term
        y_buf[rows, :] = y.astype(BF16)

    def merge_project(t):
        rows = slice(t * sub, (t + 1) * sub)
        pa = jnp.dot(y_buf[rows, :], wa_ref[...], preferred_element_type=F32)
        ph = jnp.dot(yh_ref[rows, :], wh_ref[...], preferred_element_type=F32)
        merged_buf[rows, :] = (gate_ref[rows, 0:D_MODEL].astype(F32) * pa
                               + gate_ref[rows, D_MODEL:GATE_WIDTH].astype(F32) * ph).astype(BF16)
        z = jnp.dot(merged_buf[rows, :], wo_ref[...], preferred_element_type=F32)
        out_ref[rows, :] = x_ref[rows, :] + _rms(z, nw_ref[...])

    for t in range(tm // sub):
        for c in range(chunks_per_sub):
            hgrn_chunk(t * chunks_per_sub + c)
            if c == 0:
                merge_weights(t)
        merge_project(t)


def _mix(os, stats, hgrn_cols, hnw, gates, x2, wa, wh, wo, nw, seq, tm):
    n = x2.shape[0]
    tps = seq // tm
    row = lambda width: pl.BlockSpec((tm, width), lambda i: (i, 0))
    grouped = lambda width: [pl.BlockSpec((None, d, tm // d, width), lambda i: (i // tps, 0, i % tps, 0))
                             for d in ATTN_DILATIONS]
    head_of_col = np.arange(ATTN_OUT) // HEAD_DIM
    head_of_lane = np.where(np.arange(LANES) < 2 * HEADS_PER_GROUP, np.arange(LANES) % HEADS_PER_GROUP, -1)
    expand = jnp.asarray(head_of_lane[:, None] == head_of_col[None, :], dtype=BF16)
    return pl.pallas_call(
        functools.partial(_mix_kernel, tm=tm, tiles_per_seq=tps),
        grid=(n // tm,),
        in_specs=grouped(ATTN_OUT) + grouped(LANES) + [row(HGRN_WIDTH)] * 5 + [_resident((1, HGRN_DK)),
                  row(GATE_WIDTH), row(D_MODEL),
                  _resident((LANES, ATTN_OUT)), _resident((ATTN_OUT, D_MODEL)),
                  _resident((HGRN_WIDTH, D_MODEL)), _resident((D_MODEL, D_MODEL)), _resident((1, D_MODEL))],
        out_specs=row(D_MODEL),
        out_shape=jax.ShapeDtypeStruct((n, D_MODEL), F32),
        scratch_shapes=[pltpu.VMEM((tm // MERGE_SUB, N_GROUPS - 1, ATTN_OUT // LANES, MERGE_SUB, LANES), F32),
                        pltpu.VMEM((tm // MERGE_SUB, N_GROUPS - 1, MERGE_SUB, LANES), F32),
                        pltpu.VMEM((tm, ATTN_OUT), BF16),
                        pltpu.VMEM((tm, D_MODEL), BF16),
                        pltpu.VMEM((tm, HGRN_WIDTH), BF16),
                        pltpu.VMEM((HGRN_HEADS, HGRN_DK, HGRN_DK), F32)],
        compiler_params=pltpu.CompilerParams(dimension_semantics=("arbitrary",),
                                             vmem_limit_bytes=VMEM_LIMIT),
        name="mix_out",
    )(*os, *stats, *hgrn_cols, hnw, gates, x2, expand, wa, wh, wo, nw)


FFN_CHUNK = 256
FFN_SUB = 256
DOWN_PIECE = 256
CONV_ROWS = 64
HALO = SUBLANES
UBUF_SLOTS = 4


def _ffn_kernel(x_ref, nw1_ref, wup_ref, cw_ref, cb_ref, wdn_ref, nw2_ref, out_ref,
                ubuf, carry, act, hbuf, *, tm, tiles_per_seq):
    sub = FFN_SUB
    n_sub = tm // sub
    n_chunks = D_FF // FFN_CHUNK
    n_pieces = D_MODEL // DOWN_PIECE
    first = (pl.program_id(0) % tiles_per_seq) == 0

    @pl.when(pl.program_id(0) == 0)
    def _():
        carry[...] = jnp.zeros_like(carry)

    slabs = FFN_CHUNK // LANES

    def up_chunk(s, c):
        for half in range(2):
            col0 = half * D_FF + c * FFN_CHUNK
            u = jnp.dot(hbuf[s], wup_ref[:, col0:col0 + FFN_CHUNK], preferred_element_type=F32)
            for sl in range(slabs):
                slot = (2 * c + half) * slabs + sl
                buf = ubuf.at[c % UBUF_SLOTS, half, sl]
                prev = carry[slot]
                buf[0:HALO, :] = jnp.where(first, 0.0, prev) if s == 0 else prev
                buf[HALO:HALO + sub, :] = u[:, sl * LANES:(sl + 1) * LANES]
                carry[slot] = buf[sub:HALO + sub, :]
        for sl in range(slabs):
            for rb in range(sub // CONV_ROWS):
                conv = []
                for half in range(2):
                    col0 = half * D_FF + c * FFN_CHUNK + sl * LANES
                    buf = ubuf.at[c % UBUF_SLOTS, half, sl]
                    w = cw_ref[:, col0:col0 + LANES]
                    r0 = HALO + rb * CONV_ROWS
                    conv.append(cb_ref[:, col0:col0 + LANES]
                                + w[0:1] * buf[r0 - 2:r0 - 2 + CONV_ROWS, :]
                                + w[1:2] * buf[r0 - 1:r0 - 1 + CONV_ROWS, :]
                                + w[2:3] * buf[r0:r0 + CONV_ROWS, :])
                gate, val = conv
                gelu = 0.5 * gate * (1.0 + lax.erf(gate * (2.0 ** -0.5)))
                act[s, rb * CONV_ROWS:(rb + 1) * CONV_ROWS,
                    c * FFN_CHUNK + sl * LANES:c * FFN_CHUNK + (sl + 1) * LANES] = (gelu * val).astype(BF16)

    piece_after_chunk = {(n_chunks * (p + 1)) // n_pieces - 1: p for p in range(n_pieces)}

    for s in range(n_sub + 1):
        if s < n_sub:
            hbuf[s] = _rms(x_ref[s * sub:(s + 1) * sub, :], nw1_ref[...]).astype(BF16)
        pieces = []
        for c in range(n_chunks):
            if s < n_sub:
                up_chunk(s, c)
            if s > 0 and c in piece_after_chunk:
                p = piece_after_chunk[c]
                pieces.append(jnp.dot(act[s - 1], wdn_ref[:, p * DOWN_PIECE:(p + 1) * DOWN_PIECE],
                                      preferred_element_type=F32))
        if s > 0:
            rows = slice((s - 1) * sub, s * sub)
            y = jnp.concatenate(pieces, axis=1)
            out_ref[rows, :] = x_ref[rows, :] + _rms(y, nw2_ref[...])


def _ffn(x2, nw1, wup, cw, cb, wdn, nw2, seq, tm):
    n = x2.shape[0]
    row = pl.BlockSpec((tm, D_MODEL), lambda i: (i, 0))
    return pl.pallas_call(
        functools.partial(_ffn_kernel, tm=tm, tiles_per_seq=seq // tm),
        grid=(n // tm,),
        in_specs=[row, _resident((1, D_MODEL)), _resident((D_MODEL, 2 * D_FF)),
                  _resident((CONV_WIDTH, 2 * D_FF)), _resident((1, 2 * D_FF)),
                  _resident((D_FF, D_MODEL)), _resident((1, D_MODEL))],
        out_specs=row,
        out_shape=jax.ShapeDtypeStruct((n, D_MODEL), F32),
        scratch_shapes=[pltpu.VMEM((UBUF_SLOTS, 2, FFN_CHUNK // LANES, HALO + FFN_SUB, LANES), F32),
                        pltpu.VMEM((2 * D_FF // LANES, HALO, LANES), F32),
                        pltpu.VMEM((tm // FFN_SUB, FFN_SUB, D_FF), BF16),
                        pltpu.VMEM((tm // FFN_SUB, FFN_SUB, D_MODEL), BF16)],
        compiler_params=pltpu.CompilerParams(dimension_semantics=("arbitrary",),
                                             vmem_limit_bytes=VMEM_LIMIT),
        name="conv_ffn",
    )(x2, nw1, wup, cw, cb, wdn, nw2)


def kernel(x, pre_mix_norm, w_in, rel_bias, hgrn_lb_raw, hgrn_norm, w_branch_attn, w_branch_hgrn,
           w_out, post_mix_norm, pre_ffn_norm, w_up, conv_w, conv_b, w_down, post_ffn_norm):
    bsz, seq, _ = x.shape
    depth = w_in.shape[0]
    n = bsz * seq
    lbs = jnp.cumsum(jax.nn.softmax(hgrn_lb_raw.astype(F32), axis=0), axis=0)
    x2 = x.reshape(n, D_MODEL)
    bias = _attn_bias(rel_bias)
    for l in range(depth):
        *qkvs, hq, hg, hk, hv, og, gates = _in_proj(
            x2, pre_mix_norm[l][None], w_in[l].astype(BF16), lbs[l][None], bsz, seq, tm=512)
        os, stats = [], []
        for g in range(N_GROUPS):
            o, stat = _attention_group(qkvs[g], bias, g)
            os.append(o)
            stats.append(stat)
        x2 = _mix(os, stats, (hq, hg, hk, hv, og), hgrn_norm[l][None], gates, x2,
                  w_branch_attn[l].astype(BF16), w_branch_hgrn[l].astype(BF16),
                  w_out[l].astype(BF16), post_mix_norm[l][None], seq, tm=512)
        x2 = _ffn(x2, pre_ffn_norm[l][None], w_up[l].astype(BF16), conv_w[l], conv_b[l][None],
                  w_down[l].astype(BF16), post_ffn_norm[l][None], seq, tm=512)
    return x2.reshape(bsz, seq, D_MODEL)
```

```python
import functools
import math

import numpy as np
import jax
import jax.numpy as jnp
from jax import lax
from jax.experimental import pallas as pl
from jax.experimental.pallas import tpu as pltpu

F32 = jnp.float32
BF16 = jnp.bfloat16

D_MODEL = 1024
ATTN_WINDOWS = (128, 512, 2048)
ATTN_DILATIONS = (1, 4, 16)
N_GROUPS = 3
HEAD_DIM = 64
HEADS_PER_GROUP = 8
ATTN_OUT = HEADS_PER_GROUP * HEAD_DIM
GROUP_QKV = 3 * ATTN_OUT
ATTN_QKV = N_GROUPS * GROUP_QKV
ATTN_BLOCK = 128
NEG_INF = -1e30
NUM_BUCKETS = 32
MAX_EXACT = 16
MAX_DISTANCE = 2048
HGRN_HEADS = 4
HGRN_DK = 128
HGRN_WIDTH = HGRN_HEADS * HGRN_DK
HGRN_CHUNK = 128
HGRN_SUB = 64
GATE_WIDTH = 2 * D_MODEL
IN_WIDTH = ATTN_QKV + 4 * HGRN_WIDTH + GATE_WIDTH
D_FF = 2816
CONV_WIDTH = 3
NORM_EPS = 1e-6
LOG2E = math.log2(math.e)

assert all(w // d == ATTN_BLOCK for w, d in zip(ATTN_WINDOWS, ATTN_DILATIONS))

LANES = 128
SUBLANES = 8
V7X_VMEM_BYTES = 64 * 1024 * 1024
VMEM_LIMIT = V7X_VMEM_BYTES * 7 // 8
ROW_TILE = 512

NT_DIMS = (((1,), (1,)), ((), ()))
TN_DIMS = (((0,), (0,)), ((), ()))


def _rms(xf, w):
    return xf * lax.rsqrt(jnp.mean(xf * xf, axis=-1, keepdims=True) + NORM_EPS) * w


def _resident(shape):
    nd = len(shape)
    return pl.BlockSpec(shape, lambda *_: (0,) * nd, pipeline_mode=pl.Buffered(1))


IN_SUB = 256
GATE_CHUNK = 512


def _residue_major_perm(rows, d):
    t = np.arange(rows)
    p = np.zeros((rows, rows), np.float32)
    p[(t % d) * (rows // d) + t // d, t] = 1.0
    return p


def _in_proj_kernel(x_ref, nw_ref, w_ref, lb_ref, pm_ref,
                    qkv0_ref, qkv1_ref, qkv2_ref, hq_ref, hg_ref, hk_ref, hv_ref, og_ref, gate_ref, *, tm):
    sub = IN_SUB
    for t in range(tm // sub):
        rows = slice(t * sub, (t + 1) * sub)
        h = _rms(x_ref[rows, :], nw_ref[...]).astype(BF16)

        def proj(lhs, c0, width):
            return jnp.dot(lhs, w_ref[:, c0:c0 + width], preferred_element_type=F32)

        for g, out_ref in enumerate((qkv0_ref, qkv1_ref, qkv2_ref)):
            d = ATTN_DILATIONS[g]
            h_g = h if d == 1 else jnp.dot(pm_ref[g - 1], h, preferred_element_type=F32).astype(BF16)
            per = sub // d
            for j in range(3):
                cols = slice(j * ATTN_OUT, (j + 1) * ATTN_OUT)
                acc = proj(h_g, (3 * g + j) * ATTN_OUT, ATTN_OUT)
                if j == 0:
                    acc = acc * (HEAD_DIM ** -0.5 * LOG2E)
                acc = acc.astype(BF16)
                for r in range(d):
                    out_ref[r, t * per:(t + 1) * per, cols] = acc[r * per:(r + 1) * per, :]

        base = ATTN_QKV
        hq_ref[rows, :] = jax.nn.silu(proj(h, base, HGRN_WIDTH)).astype(BF16)
        lb = lb_ref[...]
        f = lb + (1.0 - lb) * jax.nn.sigmoid(proj(h, base + HGRN_WIDTH, HGRN_WIDTH))
        hg_ref[rows, :] = jnp.log(f)
        hk_ref[rows, :] = (1.0 - f).astype(BF16)
        hv_ref[rows, :] = proj(h, base + 2 * HGRN_WIDTH, HGRN_WIDTH).astype(BF16)
        og_ref[rows, :] = jax.nn.silu(proj(h, base + 3 * HGRN_WIDTH, HGRN_WIDTH)).astype(BF16)

        base = ATTN_QKV + 4 * HGRN_WIDTH
        for c in range(GATE_WIDTH // GATE_CHUNK):
            cols = slice(c * GATE_CHUNK, (c + 1) * GATE_CHUNK)
            gate_ref[rows, cols] = jax.nn.sigmoid(proj(h, base + c * GATE_CHUNK, GATE_CHUNK)).astype(BF16)


def _in_proj(x2, nw, w_bf, lb, bsz, seq, tm):
    n = x2.shape[0]
    tps = seq // tm
    row = lambda width: pl.BlockSpec((tm, width), lambda i: (i, 0))
    f32_out = jax.ShapeDtypeStruct((n, HGRN_WIDTH), F32)
    bf16_out = jax.ShapeDtypeStruct((n, HGRN_WIDTH), BF16)
    qkv_specs = [pl.BlockSpec((None, d, tm // d, GROUP_QKV), lambda i: (i // tps, 0, i % tps, 0))
                 for d in ATTN_DILATIONS]
    qkv_shapes = [jax.ShapeDtypeStruct((bsz, d, seq // d, GROUP_QKV), BF16) for d in ATTN_DILATIONS]
    perms = jnp.asarray(np.stack([_residue_major_perm(IN_SUB, d) for d in ATTN_DILATIONS[1:]]), dtype=BF16)
    return pl.pallas_call(
        functools.partial(_in_proj_kernel, tm=tm),
        grid=(n // tm,),
        in_specs=[row(D_MODEL), _resident((1, D_MODEL)), _resident((D_MODEL, IN_WIDTH)),
                  _resident((1, HGRN_WIDTH)), _resident((N_GROUPS - 1, IN_SUB, IN_SUB))],
        out_specs=qkv_specs + [row(HGRN_WIDTH)] * 5 + [row(GATE_WIDTH)],
        out_shape=qkv_shapes + [bf16_out, f32_out, bf16_out, bf16_out, bf16_out]
                  + [jax.ShapeDtypeStruct((n, GATE_WIDTH), BF16)],
        compiler_params=pltpu.CompilerParams(dimension_semantics=("arbitrary",),
                                             vmem_limit_bytes=VMEM_LIMIT),
        name="in_proj",
    )(x2, nw, w_bf, lb, perms)


def _bucket_table(dilation):
    blk = ATTN_BLOCK
    rel = np.arange(blk)[:, None] + blk - np.arange(2 * blk)[None, :]
    dist = np.maximum(rel * dilation, 0)
    nf = np.maximum(dist, 1).astype(np.float32)
    large = MAX_EXACT + (np.log(nf / np.float32(MAX_EXACT)) / np.float32(math.log(MAX_DISTANCE / MAX_EXACT))
                         * np.float32(NUM_BUCKETS - MAX_EXACT)).astype(np.int32)
    large = np.minimum(large, NUM_BUCKETS - 1)
    bucket = np.where(dist < MAX_EXACT, dist, large)
    in_win = (rel >= 0) & (rel <= blk)
    return np.where(in_win, bucket, -1).astype(np.int32)


def _bias_kernel(tab_ref, bucket_ref, bias_ref):
    h = pl.program_id(0)
    bucket = bucket_ref[0]
    acc = jnp.full(bucket.shape, NEG_INF, F32)
    for b in range(NUM_BUCKETS):
        acc = jnp.where(bucket == b, tab_ref[b, h] * LOG2E, acc)
    bias_ref[0, 0] = acc
    col = lax.broadcasted_iota(jnp.int32, bucket.shape, 1)
    bias_ref[1, 0] = jnp.where(col >= ATTN_BLOCK, acc, NEG_INF)


def _attn_bias(rel_bias):
    blk = ATTN_BLOCK
    n_heads = N_GROUPS * HEADS_PER_GROUP
    buckets = jnp.asarray(np.stack([_bucket_table(d) for d in ATTN_DILATIONS]))
    return pl.pallas_call(
        _bias_kernel,
        grid=(n_heads,),
        in_specs=[pl.BlockSpec(memory_space=pltpu.SMEM),
                  pl.BlockSpec((1, blk, 2 * blk), lambda h: (h // HEADS_PER_GROUP, 0, 0))],
        out_specs=pl.BlockSpec((2, 1, blk, 2 * blk), lambda h: (0, h, 0, 0)),
        out_shape=jax.ShapeDtypeStruct((2, n_heads, blk, 2 * blk), F32),
        compiler_params=pltpu.CompilerParams(dimension_semantics=("arbitrary",)),
        name="attn_bias",
    )(rel_bias, buckets)


ATTN_ROWS_PER_STEP = 2048


STAT_L_LANE = 64


def _attn_kernel(q_ref, kp_ref, kc_ref, vp_ref, vc_ref, bias_ref, o_ref, stat_ref, kbuf, vbuf, *, nres, rows):
    blk = ATTN_BLOCK
    pairs = HEADS_PER_GROUP // 2
    n = pl.program_id(2)
    ones = jnp.ones((blk + rows, LANES), BF16)
    for r in range(nres):
        kbuf[r, 0:blk] = kp_ref[r]
        kbuf[r, blk:blk + rows] = kc_ref[r]
        for pair in range(pairs):
            cols = slice(pair * LANES, (pair + 1) * LANES)
            vbuf[r, pair, 0:blk, 0:LANES] = vp_ref[r, :, cols]
            vbuf[r, pair, blk:blk + rows, 0:LANES] = vc_ref[r, :, cols]
            vbuf[r, pair, :, LANES:2 * LANES] = ones

    lane = lax.broadcasted_iota(jnp.int32, (blk, LANES), 1)
    low = lane < HEAD_DIM
    head_mask = (jnp.where(low, 1.0, 0.0).astype(BF16), jnp.where(low, 0.0, 1.0).astype(BF16))

    for r in range(nres):
        for i in range(rows // blk):
            variant = jnp.where(n == 0, 1, 0) if i == 0 else 0
            stat = jnp.where(lane < STAT_L_LANE, 0.0, 1.0)
            for pair in range(pairs):
                cols = slice(pair * LANES, (pair + 1) * LANES)
                q = q_ref[r, i * blk:(i + 1) * blk, cols]
                kk = kbuf[r, i * blk:(i + 2) * blk, cols]
                q2 = jnp.concatenate([q * head_mask[0], q * head_mask[1]], axis=0)
                s = lax.dot_general(q2, kk, NT_DIMS, preferred_element_type=F32)
                s = s + jnp.concatenate([bias_ref[variant, 2 * pair], bias_ref[variant, 2 * pair + 1]], axis=0)
                m = jnp.max(s, axis=-1, keepdims=True)
                p = jnp.exp2(s - m)
                pv = jnp.dot(p.astype(BF16), vbuf[r, pair, i * blk:(i + 2) * blk, :],
                             preferred_element_type=F32)
                o_ref[r, i * blk:(i + 1) * blk, cols] = jnp.where(
                    low, pv[0:blk, 0:LANES], pv[blk:2 * blk, 0:LANES]).astype(o_ref.dtype)
                for hh in range(2):
                    head = 2 * pair + hh
                    stat = jnp.where(lane == head, m[hh * blk:(hh + 1) * blk], stat)
                    stat = jnp.where(lane == STAT_L_LANE + head,
                                     pv[hh * blk:(hh + 1) * blk, LANES:2 * LANES], stat)
            stat_ref[r, i * blk:(i + 1) * blk, :] = stat


def _attention_group(qkv, bias, g):
    bsz, d, u, _ = qkv.shape
    blk = ATTN_BLOCK
    rows = min(ATTN_ROWS_PER_STEP, u)
    nres = min(ATTN_ROWS_PER_STEP // rows, d)
    nb = u // rows

    def cur(j):
        return pl.BlockSpec((None, nres, rows, ATTN_OUT), lambda b, r, n: (b, r, n, j))

    def prev(j):
        return pl.BlockSpec((None, nres, blk, ATTN_OUT),
                            lambda b, r, n: (b, r, jnp.maximum(n * (rows // blk) - 1, 0), j))

    stat = pl.BlockSpec((None, nres, rows, LANES), lambda b, r, n: (b, r, n, 0))
    stat_shape = jax.ShapeDtypeStruct((bsz, d, u, LANES), F32)
    return pl.pallas_call(
        functools.partial(_attn_kernel, nres=nres, rows=rows),
        grid=(bsz, d // nres, nb),
        in_specs=[cur(0), prev(1), cur(1), prev(2), cur(2),
                  pl.BlockSpec((2, HEADS_PER_GROUP, blk, 2 * blk), lambda b, r, n: (0, g, 0, 0),
                               pipeline_mode=pl.Buffered(1))],
        out_specs=[pl.BlockSpec((None, nres, rows, ATTN_OUT), lambda b, r, n: (b, r, n, 0)), stat],
        out_shape=[jax.ShapeDtypeStruct((bsz, d, u, ATTN_OUT), BF16), stat_shape],
        scratch_shapes=[pltpu.VMEM((nres, blk + rows, ATTN_OUT), BF16),
                        pltpu.VMEM((nres, HEADS_PER_GROUP // 2, blk + rows, 2 * LANES), BF16)],
        compiler_params=pltpu.CompilerParams(dimension_semantics=("arbitrary",) * 3,
                                             vmem_limit_bytes=VMEM_LIMIT),
        name=f"attn_g{g}",
    )(qkv, qkv, qkv, qkv, qkv, bias)


def _hgrn_chunks(q_ref, g_ref, k_ref, v_ref, og_ref, nw_ref, o_ref, st_ref):
    c_len, h_len = HGRN_CHUNK, HGRN_SUB
    r_i = lax.broadcasted_iota(jnp.int32, (c_len, c_len), 0)
    c_i = lax.broadcasted_iota(jnp.int32, (c_len, c_len), 1)
    tril = r_i >= c_i
    ones_tril = jnp.where(tril, 1.0, 0.0).astype(BF16)
    same_sub = jnp.logical_and(tril, c_i >= (r_i // h_len) * h_len)
    first_sub = lax.broadcasted_iota(jnp.int32, (c_len, HGRN_DK), 0) < h_len
    no_keys = jnp.zeros((h_len, HGRN_DK), BF16)
    nw = nw_ref[...]

    def chunk(c):
        sl = slice(c * c_len, (c + 1) * c_len)
        g = g_ref[sl, :]
        g1 = g.astype(BF16)
        g2 = (g - g1.astype(F32)).astype(BF16)
        cum_all = (jnp.dot(ones_tril, g1, preferred_element_type=F32)
                   + jnp.dot(ones_tril, g2, preferred_element_type=F32))
        for hd in range(HGRN_HEADS):
            cols = slice(hd * HGRN_DK, (hd + 1) * HGRN_DK)
            cum = cum_all[:, cols]
            mid0 = cum[h_len // 2 - 1:h_len // 2, :]
            mid1 = cum[h_len + h_len // 2 - 1:h_len + h_len // 2, :]
            edge = cum[h_len - 1:h_len, :]
            last = cum[c_len - 1:c_len, :]
            q = q_ref[sl, cols].astype(F32)
            k = k_ref[sl, cols].astype(F32)
            v = v_ref[sl, cols]
            rel = cum - jnp.where(first_sub, mid0, mid1)
            q_m = (q * jnp.exp(rel)).astype(BF16)
            k_m = (k * jnp.exp(-rel)).astype(BF16)
            a = jnp.where(same_sub, lax.dot_general(q_m, k_m, NT_DIMS, preferred_element_type=F32), 0.0)
            q_x = (q[h_len:] * jnp.exp(cum[h_len:] - edge)).astype(BF16)
            k_x = jnp.concatenate([(k[:h_len] * jnp.exp(edge - cum[:h_len])).astype(BF16), no_keys], axis=0)
            cross = lax.dot_general(q_x, k_x, NT_DIMS, preferred_element_type=F32)
            a = jnp.concatenate([a[:h_len], a[h_len:] + cross], axis=0).astype(BF16)
            state_t = st_ref[hd]
            q_d = (q * jnp.exp(cum)).astype(BF16)
            o = (jnp.dot(a, v, preferred_element_type=F32)
                 + lax.dot_general(q_d, state_t.astype(BF16), NT_DIMS, preferred_element_type=F32))
            k_d = (k * jnp.exp(last - cum)).astype(BF16)
            st_ref[hd] = state_t * jnp.exp(last) + lax.dot_general(v, k_d, TN_DIMS, preferred_element_type=F32)
            y = _rms(o, nw) * og_ref[sl, cols].astype(F32)
            o_ref[sl, cols] = y.astype(BF16)

    return chunk


MERGE_SUB = 256


def _mix_kernel(o1_ref, o2_ref, o3_ref, s1_ref, s2_ref, s3_ref,
                hq_ref, hg_ref, hk_ref, hv_ref, og_ref, hnw_ref,
                gate_ref, x_ref, expand_ref, wa_ref, wh_ref, wo_ref, nw_ref, out_ref,
                o_tok, s_tok, y_buf, merged_buf, yh_ref, st_ref, *, tm, tiles_per_seq):
    @pl.when(pl.program_id(0) % tiles_per_seq == 0)
    def _():
        st_ref[...] = jnp.zeros_like(st_ref)

    hgrn_chunk = _hgrn_chunks(hq_ref, hg_ref, hk_ref, hv_ref, og_ref, hnw_ref, yh_ref, st_ref)

    slabs = ATTN_OUT // LANES
    sub = MERGE_SUB
    chunks_per_sub = sub // HGRN_CHUNK
    lane = lax.broadcasted_iota(jnp.int32, (sub, LANES), 1)
    copy_lanes = (lane % STAT_L_LANE) // HEADS_PER_GROUP == 1
    expand = expand_ref[...]

    def merge_weights(t):
        rows = slice(t * sub, (t + 1) * sub)
        os = [o1_ref[0, rows, :].astype(F32)]
        stats = [s1_ref[0, rows, :]]
        for g, (o_ref, s_ref) in enumerate(((o2_ref, s2_ref), (o3_ref, s3_ref))):
            d = ATTN_DILATIONS[g + 1]
            per = sub // d
            for r in range(d):
                tok_rows = pl.ds(r, per, stride=d)
                s_tok[t, g, tok_rows, :] = s_ref[r, t * per:(t + 1) * per, :]
                for s in range(slabs):
                    o_tok[t, g, s, tok_rows, :] = (
                        o_ref[r, t * per:(t + 1) * per, s * LANES:(s + 1) * LANES].astype(F32))
            os.append(jnp.concatenate([o_tok[t, g, s] for s in range(slabs)], axis=1))
            stats.append(s_tok[t, g])
        stats = [jnp.where(copy_lanes, pltpu.roll(st, HEADS_PER_GROUP, axis=1), st) for st in stats]
        mx = jnp.maximum(jnp.maximum(stats[0], stats[1]), stats[2])
        es = [jnp.exp2(st - mx) for st in stats]
        ls = [pltpu.roll(st, STAT_L_LANE, axis=1) for st in stats]
        den = ls[0] * es[0] + ls[1] * es[1] + ls[2] * es[2]
        y = None
        for e, o in zip(es, os):
            w = e / den
            w_hi = w.astype(BF16).astype(F32)
            w_split = jnp.where(lane < HEADS_PER_GROUP, w_hi, w - w_hi)
            w_split = jnp.where(lane < 2 * HEADS_PER_GROUP, w_split, 0.0).astype(BF16)
            term = jnp.dot(w_split, expand, preferred_element_type=F32) * o
            y = term if y is None else y + term
        y_buf[rows, :] = y.astype(BF16)

    def merge_project(t):
        rows = slice(t * sub, (t + 1) * sub)
        pa = jnp.dot(y_buf[rows, :], wa_ref[...], preferred_element_type=F32)
        ph = jnp.dot(yh_ref[rows, :], wh_ref[...], preferred_element_type=F32)
        merged_buf[rows, :] = (gate_ref[rows, 0:D_MODEL].astype(F32) * pa
                               + gate_ref[rows, D_MODEL:GATE_WIDTH].astype(F32) * ph).astype(BF16)
        z = jnp.dot(merged_buf[rows, :], wo_ref[...], preferred_element_type=F32)
        out_ref[rows, :] = x_ref[rows, :] + _rms(z, nw_ref[...])

    for t in range(tm // sub):
        for c in range(chunks_per_sub):
            hgrn_chunk(t * chunks_per_sub + c)
            if c == 0:
                merge_weights(t)
        merge_project(t)


def _mix(os, stats, hgrn_cols, hnw, gates, x2, wa, wh, wo, nw, seq, tm):
    n = x2.shape[0]
    tps = seq // tm
    row = lambda width: pl.BlockSpec((tm, width), lambda i: (i, 0))
    grouped = lambda width: [pl.BlockSpec((None, d, tm // d, width), lambda i: (i // tps, 0, i % tps, 0))
                             for d in ATTN_DILATIONS]
    head_of_col = np.arange(ATTN_OUT) // HEAD_DIM
    head_of_lane = np.where(np.arange(LANES) < 2 * HEADS_PER_GROUP, np.arange(LANES) % HEADS_PER_GROUP, -1)
    expand = jnp.asarray(head_of_lane[:, None] == head_of_col[None, :], dtype=BF16)
    return pl.pallas_call(
        functools.partial(_mix_kernel, tm=tm, tiles_per_seq=tps),
        grid=(n // tm,),
        in_specs=grouped(ATTN_OUT) + grouped(LANES) + [row(HGRN_WIDTH)] * 5 + [_resident((1, HGRN_DK)),
                  row(GATE_WIDTH), row(D_MODEL),
                  _resident((LANES, ATTN_OUT)), _resident((ATTN_OUT, D_MODEL)),
                  _resident((HGRN_WIDTH, D_MODEL)), _resident((D_MODEL, D_MODEL)), _resident((1, D_MODEL))],
        out_specs=row(D_MODEL),
        out_shape=jax.ShapeDtypeStruct((n, D_MODEL), F32),
        scratch_shapes=[pltpu.VMEM((tm // MERGE_SUB, N_GROUPS - 1, ATTN_OUT // LANES, MERGE_SUB, LANES), F32),
                        pltpu.VMEM((tm // MERGE_SUB, N_GROUPS - 1, MERGE_SUB, LANES), F32),
                        pltpu.VMEM((tm, ATTN_OUT), BF16),
                        pltpu.VMEM((tm, D_MODEL), BF16),
                        pltpu.VMEM((tm, HGRN_WIDTH), BF16),
                        pltpu.VMEM((HGRN_HEADS, HGRN_DK, HGRN_DK), F32)],
        compiler_params=pltpu.CompilerParams(dimension_semantics=("arbitrary",),
                                             vmem_limit_bytes=VMEM_LIMIT),
        name="mix_out",
    )(*os, *stats, *hgrn_cols, hnw, gates, x2, expand, wa, wh, wo, nw)


FFN_CHUNK = 256
FFN_SUB = 512
DOWN_PIECE = 256
CONV_ROWS = 64
HALO = SUBLANES
UBUF_SLOTS = 4


def _ffn_kernel(x_ref, nw1_ref, wup_ref, cw_ref, cb_ref, wdn_ref, nw2_ref, out_ref,
                ubuf, carry, act, hbuf, *, tm, tiles_per_seq):
    sub = FFN_SUB
    n_sub = tm // sub
    n_chunks = D_FF // FFN_CHUNK
    n_pieces = D_MODEL // DOWN_PIECE
    first = (pl.program_id(0) % tiles_per_seq) == 0

    @pl.when(pl.program_id(0) == 0)
    def _():
        carry[...] = jnp.zeros_like(carry)

    slabs = FFN_CHUNK // LANES

    def up_chunk(s, c):
        for half in range(2):
            col0 = half * D_FF + c * FFN_CHUNK
            u = jnp.dot(hbuf[s], wup_ref[:, col0:col0 + FFN_CHUNK], preferred_element_type=F32)
            for sl in range(slabs):
                slot = (2 * c + half) * slabs + sl
                buf = ubuf.at[c % UBUF_SLOTS, half, sl]
                prev = carry[slot]
                buf[0:HALO, :] = jnp.where(first, 0.0, prev) if s == 0 else prev
                buf[HALO:HALO + sub, :] = u[:, sl * LANES:(sl + 1) * LANES]
                carry[slot] = buf[sub:HALO + sub, :]
        for sl in range(slabs):
            for rb in range(sub // CONV_ROWS):
                conv = []
                for half in range(2):
                    col0 = half * D_FF + c * FFN_CHUNK + sl * LANES
                    buf = ubuf.at[c % UBUF_SLOTS, half, sl]
                    w = cw_ref[:, col0:col0 + LANES]
                    r0 = HALO + rb * CONV_ROWS
                    conv.append(cb_ref[:, col0:col0 + LANES]
                                + w[0:1] * buf[r0 - 2:r0 - 2 + CONV_ROWS, :]
                                + w[1:2] * buf[r0 - 1:r0 - 1 + CONV_ROWS, :]
                                + w[2:3] * buf[r0:r0 + CONV_ROWS, :])
                gate, val = conv
                gelu = 0.5 * gate * (1.0 + lax.erf(gate * (2.0 ** -0.5)))
                act[s, rb * CONV_ROWS:(rb + 1) * CONV_ROWS,
                    c * FFN_CHUNK + sl * LANES:c * FFN_CHUNK + (sl + 1) * LANES] = (gelu * val).astype(BF16)

    piece_after_chunk = {(n_chunks * (p + 1)) // n_pieces - 1: p for p in range(n_pieces)}

    for s in range(n_sub + 1):
        if s < n_sub:
            hbuf[s] = _rms(x_ref[s * sub:(s + 1) * sub, :], nw1_ref[...]).astype(BF16)
        pieces = []
        for c in range(n_chunks):
            if s < n_sub:
                up_chunk(s, c)
            if s > 0 and c in piece_after_chunk:
                p = piece_after_chunk[c]
                pieces.append(jnp.dot(act[s - 1], wdn_ref[:, p * DOWN_PIECE:(p + 1) * DOWN_PIECE],
                                      preferred_element_type=F32))
        if s > 0:
            rows = slice((s - 1) * sub, s * sub)
            y = jnp.concatenate(pieces, axis=1)
            out_ref[rows, :] = x_ref[rows, :] + _rms(y, nw2_ref[...])


def _ffn(x2, nw1, wup, cw, cb, wdn, nw2, seq, tm):
    n = x2.shape[0]
    row = pl.BlockSpec((tm, D_MODEL), lambda i: (i, 0))
    return pl.pallas_call(
        functools.partial(_ffn_kernel, tm=tm, tiles_per_seq=seq // tm),
        grid=(n // tm,),
        in_specs=[row, _resident((1, D_MODEL)), _resident((D_MODEL, 2 * D_FF)),
                  _resident((CONV_WIDTH, 2 * D_FF)), _resident((1, 2 * D_FF)),
                  _resident((D_FF, D_MODEL)), _resident((1, D_MODEL))],
        out_specs=row,
        out_shape=jax.ShapeDtypeStruct((n, D_MODEL), F32),
        scratch_shapes=[pltpu.VMEM((UBUF_SLOTS, 2, FFN_CHUNK // LANES, HALO + FFN_SUB, LANES), F32),
                        pltpu.VMEM((2 * D_FF // LANES, HALO, LANES), F32),
                        pltpu.VMEM((tm // FFN_SUB, FFN_SUB, D_FF), BF16),
                        pltpu.VMEM((tm // FFN_SUB, FFN_SUB, D_MODEL), BF16)],
        compiler_params=pltpu.CompilerParams(dimension_semantics=("arbitrary",),
                                             vmem_limit_bytes=VMEM_LIMIT),
        name="conv_ffn",
    )(x2, nw1, wup, cw, cb, wdn, nw2)


def kernel(x, pre_mix_norm, w_in, rel_bias, hgrn_lb_raw, hgrn_norm, w_branch_attn, w_branch_hgrn,
           w_out, post_mix_norm, pre_ffn_norm, w_up, conv_w, conv_b, w_down, post_ffn_norm):
    bsz, seq, d_model = x.shape
    depth = w_in.shape[0]
    n = bsz * seq
    assert d_model == D_MODEL and w_in.shape[1:] == (D_MODEL, IN_WIDTH) and w_up.shape[1:] == (D_MODEL, 2 * D_FF)
    assert rel_bias.shape == (NUM_BUCKETS, N_GROUPS * HEADS_PER_GROUP)
    assert seq % (max(ATTN_DILATIONS) * ATTN_BLOCK) == 0 and seq % ROW_TILE == 0
    lbs = jnp.cumsum(jax.nn.softmax(hgrn_lb_raw.astype(F32), axis=0), axis=0)
    x2 = x.reshape(n, D_MODEL)
    bias = _attn_bias(rel_bias)
    for l in range(depth):
        *qkvs, hq, hg, hk, hv, og, gates = _in_proj(
            x2, pre_mix_norm[l][None], w_in[l].astype(BF16), lbs[l][None], bsz, seq, tm=ROW_TILE)
        os, stats = [], []
        for g in range(N_GROUPS):
            o, stat = _attention_group(qkvs[g], bias, g)
            os.append(o)
            stats.append(stat)
        x2 = _mix(os, stats, (hq, hg, hk, hv, og), hgrn_norm[l][None], gates, x2,
                  w_branch_attn[l].astype(BF16), w_branch_hgrn[l].astype(BF16),
                  w_out[l].astype(BF16), post_mix_norm[l][None], seq, tm=ROW_TILE)
        x2 = _ffn(x2, pre_ffn_norm[l][None], w_up[l].astype(BF16), conv_w[l], conv_b[l][None],
                  w_down[l].astype(BF16), post_ffn_norm[l][None], seq, tm=ROW_TILE)
    return x2.reshape(bsz, seq, D_MODEL)
```

```python
import functools
import math

import numpy as np
import jax
import jax.numpy as jnp
from jax import lax
from jax.experimental import pallas as pl
from jax.experimental.pallas import tpu as pltpu

F32 = jnp.float32
BF16 = jnp.bfloat16

D_MODEL = 1024
ATTN_WINDOWS = (128, 512, 2048)
ATTN_DILATIONS = (1, 4, 16)
N_GROUPS = 3
HEAD_DIM = 64
HEADS_PER_GROUP = 8
ATTN_OUT = HEADS_PER_GROUP * HEAD_DIM
GROUP_QKV = 3 * ATTN_OUT
ATTN_QKV = N_GROUPS * GROUP_QKV
ATTN_BLOCK = 128
NEG_INF = -1e30
NUM_BUCKETS = 32
MAX_EXACT = 16
MAX_DISTANCE = 2048
HGRN_HEADS = 4
HGRN_DK = 128
HGRN_WIDTH = HGRN_HEADS * HGRN_DK
HGRN_CHUNK = 128
HGRN_SUB = 64
GATE_WIDTH = 2 * D_MODEL
IN_WIDTH = ATTN_QKV + 4 * HGRN_WIDTH + GATE_WIDTH
D_FF = 2816
CONV_WIDTH = 3
NORM_EPS = 1e-6
LOG2E = math.log2(math.e)

assert all(w // d == ATTN_BLOCK for w, d in zip(ATTN_WINDOWS, ATTN_DILATIONS))

LANES = 128
SUBLANES = 8
V7X_VMEM_BYTES = 64 * 1024 * 1024
VMEM_LIMIT = V7X_VMEM_BYTES * 7 // 8
ROW_TILE = 512

NT_DIMS = (((1,), (1,)), ((), ()))
TN_DIMS = (((0,), (0,)), ((), ()))


def _rms(xf, w):
    return xf * lax.rsqrt(jnp.mean(xf * xf, axis=-1, keepdims=True) + NORM_EPS) * w


def _resident(shape):
    nd = len(shape)
    return pl.BlockSpec(shape, lambda *_: (0,) * nd, pipeline_mode=pl.Buffered(1))


IN_SUB = 512
PERM_ROWS = 256
GATE_CHUNK = 512


def _residue_major_perm(rows, d):
    t = np.arange(rows)
    p = np.zeros((rows, rows), np.float32)
    p[(t % d) * (rows // d) + t // d, t] = 1.0
    return p


def _in_proj_kernel(x_ref, nw_ref, w_ref, lb_ref, pm_ref,
                    qkv0_ref, qkv1_ref, qkv2_ref, hq_ref, hg_ref, hk_ref, hv_ref, og_ref, gate_ref, *, tm):
    sub = IN_SUB
    for t in range(tm // sub):
        rows = slice(t * sub, (t + 1) * sub)
        h = _rms(x_ref[rows, :], nw_ref[...]).astype(BF16)

        def proj(lhs, c0, width):
            return jnp.dot(lhs, w_ref[:, c0:c0 + width], preferred_element_type=F32)

        for g, out_ref in enumerate((qkv0_ref, qkv1_ref, qkv2_ref)):
            d = ATTN_DILATIONS[g]
            per = sub // d
            if d == 1:
                h_g = h
            else:
                n_blk = sub // PERM_ROWS
                piece = PERM_ROWS // d
                blocks = [jnp.dot(pm_ref[g - 1], h[b * PERM_ROWS:(b + 1) * PERM_ROWS],
                                  preferred_element_type=F32).astype(BF16) for b in range(n_blk)]
                h_g = jnp.concatenate([blocks[b][r * piece:(r + 1) * piece]
                                       for r in range(d) for b in range(n_blk)], axis=0)
            for j in range(3):
                cols = slice(j * ATTN_OUT, (j + 1) * ATTN_OUT)
                acc = proj(h_g, (3 * g + j) * ATTN_OUT, ATTN_OUT)
                if j == 0:
                    acc = acc * (HEAD_DIM ** -0.5 * LOG2E)
                acc = acc.astype(BF16)
                for r in range(d):
                    out_ref[r, t * per:(t + 1) * per, cols] = acc[r * per:(r + 1) * per, :]

        base = ATTN_QKV
        hq_ref[rows, :] = jax.nn.silu(proj(h, base, HGRN_WIDTH)).astype(BF16)
        lb = lb_ref[...]
        f = lb + (1.0 - lb) * jax.nn.sigmoid(proj(h, base + HGRN_WIDTH, HGRN_WIDTH))
        hg_ref[rows, :] = jnp.log(f)
        hk_ref[rows, :] = (1.0 - f).astype(BF16)
        hv_ref[rows, :] = proj(h, base + 2 * HGRN_WIDTH, HGRN_WIDTH).astype(BF16)
        og_ref[rows, :] = jax.nn.silu(proj(h, base + 3 * HGRN_WIDTH, HGRN_WIDTH)).astype(BF16)

        base = ATTN_QKV + 4 * HGRN_WIDTH
        for c in range(GATE_WIDTH // GATE_CHUNK):
            cols = slice(c * GATE_CHUNK, (c + 1) * GATE_CHUNK)
            gate_ref[rows, cols] = jax.nn.sigmoid(proj(h, base + c * GATE_CHUNK, GATE_CHUNK)).astype(BF16)


def _in_proj(x2, nw, w_bf, lb, bsz, seq, tm):
    n = x2.shape[0]
    tps = seq // tm
    row = lambda width: pl.BlockSpec((tm, width), lambda i: (i, 0))
    f32_out = jax.ShapeDtypeStruct((n, HGRN_WIDTH), F32)
    bf16_out = jax.ShapeDtypeStruct((n, HGRN_WIDTH), BF16)
    qkv_specs = [pl.BlockSpec((None, d, tm // d, GROUP_QKV), lambda i: (i // tps, 0, i % tps, 0))
                 for d in ATTN_DILATIONS]
    qkv_shapes = [jax.ShapeDtypeStruct((bsz, d, seq // d, GROUP_QKV), BF16) for d in ATTN_DILATIONS]
    perms = jnp.asarray(np.stack([_residue_major_perm(PERM_ROWS, d) for d in ATTN_DILATIONS[1:]]), dtype=BF16)
    return pl.pallas_call(
        functools.partial(_in_proj_kernel, tm=tm),
        grid=(n // tm,),
        in_specs=[row(D_MODEL), _resident((1, D_MODEL)), _resident((D_MODEL, IN_WIDTH)),
                  _resident((1, HGRN_WIDTH)), _resident((N_GROUPS - 1, PERM_ROWS, PERM_ROWS))],
        out_specs=qkv_specs + [row(HGRN_WIDTH)] * 5 + [row(GATE_WIDTH)],
        out_shape=qkv_shapes + [bf16_out, f32_out, bf16_out, bf16_out, bf16_out]
                  + [jax.ShapeDtypeStruct((n, GATE_WIDTH), BF16)],
        compiler_params=pltpu.CompilerParams(dimension_semantics=("arbitrary",),
                                             vmem_limit_bytes=VMEM_LIMIT),
        name="in_proj",
    )(x2, nw, w_bf, lb, perms)


def _bucket_table(dilation):
    blk = ATTN_BLOCK
    rel = np.arange(blk)[:, None] + blk - np.arange(2 * blk)[None, :]
    dist = np.maximum(rel * dilation, 0)
    nf = np.maximum(dist, 1).astype(np.float32)
    large = MAX_EXACT + (np.log(nf / np.float32(MAX_EXACT)) / np.float32(math.log(MAX_DISTANCE / MAX_EXACT))
                         * np.float32(NUM_BUCKETS - MAX_EXACT)).astype(np.int32)
    large = np.minimum(large, NUM_BUCKETS - 1)
    bucket = np.where(dist < MAX_EXACT, dist, large)
    in_win = (rel >= 0) & (rel <= blk)
    return np.where(in_win, bucket, -1).astype(np.int32)


def _bias_kernel(tab_ref, bucket_ref, bias_ref):
    h = pl.program_id(0)
    bucket = bucket_ref[0]
    acc = jnp.full(bucket.shape, NEG_INF, F32)
    for b in range(NUM_BUCKETS):
        acc = jnp.where(bucket == b, tab_ref[b, h] * LOG2E, acc)
    bias_ref[0, 0] = acc
    col = lax.broadcasted_iota(jnp.int32, bucket.shape, 1)
    bias_ref[1, 0] = jnp.where(col >= ATTN_BLOCK, acc, NEG_INF)


def _attn_bias(rel_bias):
    blk = ATTN_BLOCK
    n_heads = N_GROUPS * HEADS_PER_GROUP
    buckets = jnp.asarray(np.stack([_bucket_table(d) for d in ATTN_DILATIONS]))
    return pl.pallas_call(
        _bias_kernel,
        grid=(n_heads,),
        in_specs=[pl.BlockSpec(memory_space=pltpu.SMEM),
                  pl.BlockSpec((1, blk, 2 * blk), lambda h: (h // HEADS_PER_GROUP, 0, 0))],
        out_specs=pl.BlockSpec((2, 1, blk, 2 * blk), lambda h: (0, h, 0, 0)),
        out_shape=jax.ShapeDtypeStruct((2, n_heads, blk, 2 * blk), F32),
        compiler_params=pltpu.CompilerParams(dimension_semantics=("arbitrary",)),
        name="attn_bias",
    )(rel_bias, buckets)


ATTN_ROWS_PER_STEP = 2048


STAT_L_LANE = 64


def _attn_kernel(q_ref, kp_ref, kc_ref, vp_ref, vc_ref, bias_ref, o_ref, stat_ref, kbuf, vbuf, *, nres, rows):
    blk = ATTN_BLOCK
    pairs = HEADS_PER_GROUP // 2
    n = pl.program_id(2)
    ones = jnp.ones((blk + rows, LANES), BF16)
    for r in range(nres):
        kbuf[r, 0:blk] = kp_ref[r]
        kbuf[r, blk:blk + rows] = kc_ref[r]
        for pair in range(pairs):
            cols = slice(pair * LANES, (pair + 1) * LANES)
            vbuf[r, pair, 0:blk, 0:LANES] = vp_ref[r, :, cols]
            vbuf[r, pair, blk:blk + rows, 0:LANES] = vc_ref[r, :, cols]
            vbuf[r, pair, :, LANES:2 * LANES] = ones

    lane = lax.broadcasted_iota(jnp.int32, (blk, LANES), 1)
    low = lane < HEAD_DIM
    head_mask = (jnp.where(low, 1.0, 0.0).astype(BF16), jnp.where(low, 0.0, 1.0).astype(BF16))

    for r in range(nres):
        for i in range(rows // blk):
            variant = jnp.where(n == 0, 1, 0) if i == 0 else 0
            stat = jnp.where(lane < STAT_L_LANE, 0.0, 1.0)
            for pair in range(pairs):
                cols = slice(pair * LANES, (pair + 1) * LANES)
                q = q_ref[r, i * blk:(i + 1) * blk, cols]
                kk = kbuf[r, i * blk:(i + 2) * blk, cols]
                q2 = jnp.concatenate([q * head_mask[0], q * head_mask[1]], axis=0)
                s = lax.dot_general(q2, kk, NT_DIMS, preferred_element_type=F32)
                s = s + jnp.concatenate([bias_ref[variant, 2 * pair], bias_ref[variant, 2 * pair + 1]], axis=0)
                m = jnp.max(s, axis=-1, keepdims=True)
                p = jnp.exp2(s - m)
                pv = jnp.dot(p.astype(BF16), vbuf[r, pair, i * blk:(i + 2) * blk, :],
                             preferred_element_type=F32)
                o_ref[r, i * blk:(i + 1) * blk, cols] = jnp.where(
                    low, pv[0:blk, 0:LANES], pv[blk:2 * blk, 0:LANES]).astype(o_ref.dtype)
                for hh in range(2):
                    head = 2 * pair + hh
                    stat = jnp.where(lane == head, m[hh * blk:(hh + 1) * blk], stat)
                    stat = jnp.where(lane == STAT_L_LANE + head,
                                     pv[hh * blk:(hh + 1) * blk, LANES:2 * LANES], stat)
            stat_ref[r, i * blk:(i + 1) * blk, :] = stat


def _attention_group(qkv, bias, g):
    bsz, d, u, _ = qkv.shape
    blk = ATTN_BLOCK
    rows = min(ATTN_ROWS_PER_STEP, u)
    nres = min(ATTN_ROWS_PER_STEP // rows, d)
    nb = u // rows

    def cur(j):
        return pl.BlockSpec((None, nres, rows, ATTN_OUT), lambda b, r, n: (b, r, n, j))

    def prev(j):
        return pl.BlockSpec((None, nres, blk, ATTN_OUT),
                            lambda b, r, n: (b, r, jnp.maximum(n * (rows // blk) - 1, 0), j))

    stat = pl.BlockSpec((None, nres, rows, LANES), lambda b, r, n: (b, r, n, 0))
    stat_shape = jax.ShapeDtypeStruct((bsz, d, u, LANES), F32)
    return pl.pallas_call(
        functools.partial(_attn_kernel, nres=nres, rows=rows),
        grid=(bsz, d // nres, nb),
        in_specs=[cur(0), prev(1), cur(1), prev(2), cur(2),
                  pl.BlockSpec((2, HEADS_PER_GROUP, blk, 2 * blk), lambda b, r, n: (0, g, 0, 0),
                               pipeline_mode=pl.Buffered(1))],
        out_specs=[pl.BlockSpec((None, nres, rows, ATTN_OUT), lambda b, r, n: (b, r, n, 0)), stat],
        out_shape=[jax.ShapeDtypeStruct((bsz, d, u, ATTN_OUT), BF16), stat_shape],
        scratch_shapes=[pltpu.VMEM((nres, blk + rows, ATTN_OUT), BF16),
                        pltpu.VMEM((nres, HEADS_PER_GROUP // 2, blk + rows, 2 * LANES), BF16)],
        compiler_params=pltpu.CompilerParams(dimension_semantics=("arbitrary",) * 3,
                                             vmem_limit_bytes=VMEM_LIMIT),
        name=f"attn_g{g}",
    )(qkv, qkv, qkv, qkv, qkv, bias)


def _hgrn_chunks(q_ref, g_ref, k_ref, v_ref, og_ref, nw_ref, o_ref, st_ref):
    c_len, h_len = HGRN_CHUNK, HGRN_SUB
    r_i = lax.broadcasted_iota(jnp.int32, (c_len, c_len), 0)
    c_i = lax.broadcasted_iota(jnp.int32, (c_len, c_len), 1)
    tril = r_i >= c_i
    ones_tril = jnp.where(tril, 1.0, 0.0).astype(BF16)
    same_sub = jnp.logical_and(tril, c_i >= (r_i // h_len) * h_len)
    first_sub = lax.broadcasted_iota(jnp.int32, (c_len, HGRN_DK), 0) < h_len
    no_keys = jnp.zeros((h_len, HGRN_DK), BF16)
    nw = nw_ref[...]

    def chunk(c):
        sl = slice(c * c_len, (c + 1) * c_len)
        g = g_ref[sl, :]
        g1 = g.astype(BF16)
        g2 = (g - g1.astype(F32)).astype(BF16)
        cum_all = (jnp.dot(ones_tril, g1, preferred_element_type=F32)
                   + jnp.dot(ones_tril, g2, preferred_element_type=F32))
        for hd in range(HGRN_HEADS):
            cols = slice(hd * HGRN_DK, (hd + 1) * HGRN_DK)
            cum = cum_all[:, cols]
            mid0 = cum[h_len // 2 - 1:h_len // 2, :]
            mid1 = cum[h_len + h_len // 2 - 1:h_len + h_len // 2, :]
            edge = cum[h_len - 1:h_len, :]
            last = cum[c_len - 1:c_len, :]
            q = q_ref[sl, cols].astype(F32)
            k = k_ref[sl, cols].astype(F32)
            v = v_ref[sl, cols]
            rel = cum - jnp.where(first_sub, mid0, mid1)
            q_m = (q * jnp.exp(rel)).astype(BF16)
            k_m = (k * jnp.exp(-rel)).astype(BF16)
            a = jnp.where(same_sub, lax.dot_general(q_m, k_m, NT_DIMS, preferred_element_type=F32), 0.0)
            q_x = (q[h_len:] * jnp.exp(cum[h_len:] - edge)).astype(BF16)
            k_x = jnp.concatenate([(k[:h_len] * jnp.exp(edge - cum[:h_len])).astype(BF16), no_keys], axis=0)
            cross = lax.dot_general(q_x, k_x, NT_DIMS, preferred_element_type=F32)
            a = jnp.concatenate([a[:h_len], a[h_len:] + cross], axis=0).astype(BF16)
            state_t = st_ref[hd]
            q_d = (q * jnp.exp(cum)).astype(BF16)
            o = (jnp.dot(a, v, preferred_element_type=F32)
                 + lax.dot_general(q_d, state_t.astype(BF16), NT_DIMS, preferred_element_type=F32))
            k_d = (k * jnp.exp(last - cum)).astype(BF16)
            st_ref[hd] = state_t * jnp.exp(last) + lax.dot_general(v, k_d, TN_DIMS, preferred_element_type=F32)
            y = _rms(o, nw) * og_ref[sl, cols].astype(F32)
            o_ref[sl, cols] = y.astype(BF16)

    return chunk


MERGE_SUB = 256


def _mix_kernel(o1_ref, o2_ref, o3_ref, s1_ref, s2_ref, s3_ref,
                hq_ref, hg_ref, hk_ref, hv_ref, og_ref, hnw_ref,
                gate_ref, x_ref, expand_ref, wa_ref, wh_ref, wo_ref, nw_ref, out_ref,
                o_tok, s_tok, y_buf, merged_buf, yh_ref, st_ref, *, tm, tiles_per_seq):
    @pl.when(pl.program_id(0) % tiles_per_seq == 0)
    def _():
        st_ref[...] = jnp.zeros_like(st_ref)

    hgrn_chunk = _hgrn_chunks(hq_ref, hg_ref, hk_ref, hv_ref, og_ref, hnw_ref, yh_ref, st_ref)

    slabs = ATTN_OUT // LANES
    sub = MERGE_SUB
    chunks_per_sub = sub // HGRN_CHUNK
    lane = lax.broadcasted_iota(jnp.int32, (sub, LANES), 1)
    copy_lanes = (lane % STAT_L_LANE) // HEADS_PER_GROUP == 1
    expand = expand_ref[...]

    def merge_weights(t):
        rows = slice(t * sub, (t + 1) * sub)
        os = [o1_ref[0, rows, :].astype(F32)]
        stats = [s1_ref[0, rows, :]]
        for g, (o_ref, s_ref) in enumerate(((o2_ref, s2_ref), (o3_ref, s3_ref))):
            d = ATTN_DILATIONS[g + 1]
            per = sub // d
            for r in range(d):
                tok_rows = pl.ds(r, per, stride=d)
                s_tok[t, g, tok_rows, :] = s_ref[r, t * per:(t + 1) * per, :]
                for s in range(slabs):
                    o_tok[t, g, s, tok_rows, :] = (
                        o_ref[r, t * per:(t + 1) * per, s * LANES:(s + 1) * LANES].astype(F32))
            os.append(jnp.concatenate([o_tok[t, g, s] for s in range(slabs)], axis=1))
            stats.append(s_tok[t, g])
        stats = [jnp.where(copy_lanes, pltpu.roll(st, HEADS_PER_GROUP, axis=1), st) for st in stats]
        mx = jnp.maximum(jnp.maximum(stats[0], stats[1]), stats[2])
        es = [jnp.exp2(st - mx) for st in stats]
        ls = [pltpu.roll(st, STAT_L_LANE, axis=1) for st in stats]
        den = ls[0] * es[0] + ls[1] * es[1] + ls[2] * es[2]
        y = None
        for e, o in zip(es, os):
            w = e / den
            w_hi = w.astype(BF16).astype(F32)
            w_split = jnp.where(lane < HEADS_PER_GROUP, w_hi, w - w_hi)
            w_split = jnp.where(lane < 2 * HEADS_PER_GROUP, w_split, 0.0).astype(BF16)
            term = jnp.dot(w_split, expand, preferred_element_type=F32) * o
            y = term if y is None else y + term
        y_buf[rows, :] = y.astype(BF16)

    def merge_project(t):
        rows = slice(t * sub, (t + 1) * sub)
        pa = jnp.dot(y_buf[rows, :], wa_ref[...], preferred_element_type=F32)
        ph = jnp.dot(yh_ref[rows, :], wh_ref[...], preferred_element_type=F32)
        merged_buf[rows, :] = (gate_ref[rows, 0:D_MODEL].astype(F32) * pa
                               + gate_ref[rows, D_MODEL:GATE_WIDTH].astype(F32) * ph).astype(BF16)
        z = jnp.dot(merged_buf[rows, :], wo_ref[...], preferred_element_type=F32)
        out_ref[rows, :] = x_ref[rows, :] + _rms(z, nw_ref[...])

    for t in range(tm // sub):
        for c in range(chunks_per_sub):
            hgrn_chunk(t * chunks_per_sub + c)
            if c == 0:
                merge_weights(t)
        merge_project(t)


def _mix(os, stats, hgrn_cols, hnw, gates, x2, wa, wh, wo, nw, seq, tm):
    n = x2.shape[0]
    tps = seq // tm
    row = lambda width: pl.BlockSpec((tm, width), lambda i: (i, 0))
    grouped = lambda width: [pl.BlockSpec((None, d, tm // d, width), lambda i: (i // tps, 0, i % tps, 0))
                             for d in ATTN_DILATIONS]
    head_of_col = np.arange(ATTN_OUT) // HEAD_DIM
    head_of_lane = np.where(np.arange(LANES) < 2 * HEADS_PER_GROUP, np.arange(LANES) % HEADS_PER_GROUP, -1)
    expand = jnp.asarray(head_of_lane[:, None] == head_of_col[None, :], dtype=BF16)
    return pl.pallas_call(
        functools.partial(_mix_kernel, tm=tm, tiles_per_seq=tps),
        grid=(n // tm,),
        in_specs=grouped(ATTN_OUT) + grouped(LANES) + [row(HGRN_WIDTH)] * 5 + [_resident((1, HGRN_DK)),
                  row(GATE_WIDTH), row(D_MODEL),
                  _resident((LANES, ATTN_OUT)), _resident((ATTN_OUT, D_MODEL)),
                  _resident((HGRN_WIDTH, D_MODEL)), _resident((D_MODEL, D_MODEL)), _resident((1, D_MODEL))],
        out_specs=row(D_MODEL),
        out_shape=jax.ShapeDtypeStruct((n, D_MODEL), F32),
        scratch_shapes=[pltpu.VMEM((tm // MERGE_SUB, N_GROUPS - 1, ATTN_OUT // LANES, MERGE_SUB, LANES), F32),
                        pltpu.VMEM((tm // MERGE_SUB, N_GROUPS - 1, MERGE_SUB, LANES), F32),
                        pltpu.VMEM((tm, ATTN_OUT), BF16),
                        pltpu.VMEM((tm, D_MODEL), BF16),
                        pltpu.VMEM((tm, HGRN_WIDTH), BF16),
                        pltpu.VMEM((HGRN_HEADS, HGRN_DK, HGRN_DK), F32)],
        compiler_params=pltpu.CompilerParams(dimension_semantics=("arbitrary",),
                                             vmem_limit_bytes=VMEM_LIMIT),
        name="mix_out",
    )(*os, *stats, *hgrn_cols, hnw, gates, x2, expand, wa, wh, wo, nw)


FFN_CHUNK = 256
FFN_SUB = 512
DOWN_PIECE = 256
CONV_ROWS = 64
HALO = SUBLANES
UBUF_SLOTS = 4


def _ffn_kernel(x_ref, nw1_ref, wup_ref, cw_ref, cb_ref, wdn_ref, nw2_ref, out_ref,
                ubuf, carry, act, hbuf, *, tm, tiles_per_seq):
    sub = FFN_SUB
    n_sub = tm // sub
    n_chunks = D_FF // FFN_CHUNK
    n_pieces = D_MODEL // DOWN_PIECE
    first = (pl.program_id(0) % tiles_per_seq) == 0

    @pl.when(pl.program_id(0) == 0)
    def _():
        carry[...] = jnp.zeros_like(carry)

    slabs = FFN_CHUNK // LANES

    def up_chunk(s, c):
        for half in range(2):
            col0 = half * D_FF + c * FFN_CHUNK
            u = jnp.dot(hbuf[s], wup_ref[:, col0:col0 + FFN_CHUNK], preferred_element_type=F32)
            for sl in range(slabs):
                slot = (2 * c + half) * slabs + sl
                buf = ubuf.at[c % UBUF_SLOTS, half, sl]
                prev = carry[slot]
                buf[0:HALO, :] = jnp.where(first, 0.0, prev) if s == 0 else prev
                buf[HALO:HALO + sub, :] = u[:, sl * LANES:(sl + 1) * LANES]
                carry[slot] = buf[sub:HALO + sub, :]
        for sl in range(slabs):
            for rb in range(sub // CONV_ROWS):
                conv = []
                for half in range(2):
                    col0 = half * D_FF + c * FFN_CHUNK + sl * LANES
                    buf = ubuf.at[c % UBUF_SLOTS, half, sl]
                    w = cw_ref[:, col0:col0 + LANES]
                    r0 = HALO + rb * CONV_ROWS
                    conv.append(cb_ref[:, col0:col0 + LANES]
                                + w[0:1] * buf[r0 - 2:r0 - 2 + CONV_ROWS, :]
                                + w[1:2] * buf[r0 - 1:r0 - 1 + CONV_ROWS, :]
                                + w[2:3] * buf[r0:r0 + CONV_ROWS, :])
                gate, val = conv
                gelu = 0.5 * gate * (1.0 + lax.erf(gate * (2.0 ** -0.5)))
                act[s, rb * CONV_ROWS:(rb + 1) * CONV_ROWS,
                    c * FFN_CHUNK + sl * LANES:c * FFN_CHUNK + (sl + 1) * LANES] = (gelu * val).astype(BF16)

    piece_after_chunk = {(n_chunks * (p + 1)) // n_pieces - 1: p for p in range(n_pieces)}

    for s in range(n_sub + 1):
        if s < n_sub:
            hbuf[s] = _rms(x_ref[s * sub:(s + 1) * sub, :], nw1_ref[...]).astype(BF16)
        pieces = []
        for c in range(n_chunks):
            if s < n_sub:
                up_chunk(s, c)
            if s > 0 and c in piece_after_chunk:
                p = piece_after_chunk[c]
                pieces.append(jnp.dot(act[s - 1], wdn_ref[:, p * DOWN_PIECE:(p + 1) * DOWN_PIECE],
                                      preferred_element_type=F32))
        if s > 0:
            rows = slice((s - 1) * sub, s * sub)
            y = jnp.concatenate(pieces, axis=1)
            out_ref[rows, :] = x_ref[rows, :] + _rms(y, nw2_ref[...])


def _ffn(x2, nw1, wup, cw, cb, wdn, nw2, seq, tm):
    n = x2.shape[0]
    row = pl.BlockSpec((tm, D_MODEL), lambda i: (i, 0))
    return pl.pallas_call(
        functools.partial(_ffn_kernel, tm=tm, tiles_per_seq=seq // tm),
        grid=(n // tm,),
        in_specs=[row, _resident((1, D_MODEL)), _resident((D_MODEL, 2 * D_FF)),
                  _resident((CONV_WIDTH, 2 * D_FF)), _resident((1, 2 * D_FF)),
                  _resident((D_FF, D_MODEL)), _resident((1, D_MODEL))],
        out_specs=row,
        out_shape=jax.ShapeDtypeStruct((n, D_MODEL), F32),
        scratch_shapes=[pltpu.VMEM((UBUF_SLOTS, 2, FFN_CHUNK // LANES, HALO + FFN_SUB, LANES), F32),
                        pltpu.VMEM((2 * D_FF // LANES, HALO, LANES), F32),
                        pltpu.VMEM((tm // FFN_SUB, FFN_SUB, D_FF), BF16),
                        pltpu.VMEM((tm // FFN_SUB, FFN_SUB, D_MODEL), BF16)],
        compiler_params=pltpu.CompilerParams(dimension_semantics=("arbitrary",),
                                             vmem_limit_bytes=VMEM_LIMIT),
        name="conv_ffn",
    )(x2, nw1, wup, cw, cb, wdn, nw2)


def kernel(x, pre_mix_norm, w_in, rel_bias, hgrn_lb_raw, hgrn_norm, w_branch_attn, w_branch_hgrn,
           w_out, post_mix_norm, pre_ffn_norm, w_up, conv_w, conv_b, w_down, post_ffn_norm):
    bsz, seq, d_model = x.shape
    depth = w_in.shape[0]
    n = bsz * seq
    assert d_model == D_MODEL and w_in.shape[1:] == (D_MODEL, IN_WIDTH) and w_up.shape[1:] == (D_MODEL, 2 * D_FF)
    assert rel_bias.shape == (NUM_BUCKETS, N_GROUPS * HEADS_PER_GROUP)
    assert seq % (max(ATTN_DILATIONS) * ATTN_BLOCK) == 0 and seq % ROW_TILE == 0
    lbs = jnp.cumsum(jax.nn.softmax(hgrn_lb_raw.astype(F32), axis=0), axis=0)
    x2 = x.reshape(n, D_MODEL)
    bias = _attn_bias(rel_bias)
    for l in range(depth):
        *qkvs, hq, hg, hk, hv, og, gates = _in_proj(
            x2, pre_mix_norm[l][None], w_in[l].astype(BF16), lbs[l][None], bsz, seq, tm=ROW_TILE)
        os, stats = [], []
        for g in range(N_GROUPS):
            o, stat = _attention_group(qkvs[g], bias, g)
            os.append(o)
            stats.append(stat)
        x2 = _mix(os, stats, (hq, hg, hk, hv, og), hgrn_norm[l][None], gates, x2,
                  w_branch_attn[l].astype(BF16), w_branch_hgrn[l].astype(BF16),
                  w_out[l].astype(BF16), post_mix_norm[l][None], seq, tm=ROW_TILE)
        x2 = _ffn(x2, pre_ffn_norm[l][None], w_up[l].astype(BF16), conv_w[l], conv_b[l][None],
                  w_down[l].astype(BF16), post_ffn_norm[l][None], seq, tm=ROW_TILE)
    return x2.reshape(bsz, seq, D_MODEL)
```

```python
import functools
import math

import numpy as np
import jax
import jax.numpy as jnp
from jax import lax
from jax.experimental import pallas as pl
from jax.experimental.pallas import tpu as pltpu

F32 = jnp.float32
BF16 = jnp.bfloat16

D_MODEL = 1024
ATTN_WINDOWS = (128, 512, 2048)
ATTN_DILATIONS = (1, 4, 16)
N_GROUPS = 3
HEAD_DIM = 64
HEADS_PER_GROUP = 8
ATTN_OUT = HEADS_PER_GROUP * HEAD_DIM
GROUP_QKV = 3 * ATTN_OUT
ATTN_QKV = N_GROUPS * GROUP_QKV
ATTN_BLOCK = 128
NEG_INF = -1e30
NUM_BUCKETS = 32
MAX_EXACT = 16
MAX_DISTANCE = 2048
HGRN_HEADS = 4
HGRN_DK = 128
HGRN_WIDTH = HGRN_HEADS * HGRN_DK
HGRN_CHUNK = 128
HGRN_SUB = 64
GATE_WIDTH = 2 * D_MODEL
SIDE_WIDTH = 4 * HGRN_WIDTH + GATE_WIDTH
IN_WIDTH = ATTN_QKV + 4 * HGRN_WIDTH + GATE_WIDTH
D_FF = 2816
CONV_WIDTH = 3
NORM_EPS = 1e-6
LOG2E = math.log2(math.e)

assert all(w // d == ATTN_BLOCK for w, d in zip(ATTN_WINDOWS, ATTN_DILATIONS))

LANES = 128
SUBLANES = 8
V7X_VMEM_BYTES = 64 * 1024 * 1024
VMEM_LIMIT = V7X_VMEM_BYTES * 7 // 8
ROW_TILE = 512

NT_DIMS = (((1,), (1,)), ((), ()))
TN_DIMS = (((0,), (0,)), ((), ()))


def _rms(xf, w):
    return xf * lax.rsqrt(jnp.mean(xf * xf, axis=-1, keepdims=True) + NORM_EPS) * w


def _resident(shape):
    nd = len(shape)
    return pl.BlockSpec(shape, lambda *_: (0,) * nd, pipeline_mode=pl.Buffered(1))


IN_SUB = 256
GATE_CHUNK = 512


def _residue_major_perm(rows, d):
    t = np.arange(rows)
    p = np.zeros((rows, rows), np.float32)
    p[(t % d) * (rows // d) + t // d, t] = 1.0
    return p


def _side_views(side_ref):
    w = HGRN_WIDTH
    return (side_ref.at[:, 0:w], side_ref.at[:, w:2 * w], side_ref.at[:, 2 * w:3 * w],
            side_ref.at[:, 3 * w:4 * w], side_ref.at[:, 4 * w:SIDE_WIDTH])


def _in_proj_kernel(x_ref, nw_ref, w_ref, lb_ref, pm_ref,
                    qkv0_ref, qkv1_ref, qkv2_ref, hg_ref, side_ref, *, tm):
    hq_ref, hk_ref, hv_ref, og_ref, gate_ref = _side_views(side_ref)
    sub = IN_SUB
    for t in range(tm // sub):
        rows = slice(t * sub, (t + 1) * sub)
        h = _rms(x_ref[rows, :], nw_ref[...]).astype(BF16)

        def proj(lhs, c0, width):
            return jnp.dot(lhs, w_ref[:, c0:c0 + width], preferred_element_type=F32)

        for g, out_ref in enumerate((qkv0_ref, qkv1_ref, qkv2_ref)):
            d = ATTN_DILATIONS[g]
            h_g = h if d == 1 else jnp.dot(pm_ref[g - 1], h, preferred_element_type=F32).astype(BF16)
            per = sub // d
            for j in range(3):
                cols = slice(j * ATTN_OUT, (j + 1) * ATTN_OUT)
                acc = proj(h_g, (3 * g + j) * ATTN_OUT, ATTN_OUT)
                if j == 0:
                    acc = acc * (HEAD_DIM ** -0.5 * LOG2E)
                acc = acc.astype(BF16)
                for r in range(d):
                    out_ref[r, t * per:(t + 1) * per, cols] = acc[r * per:(r + 1) * per, :]

        base = ATTN_QKV
        hq_ref[rows, :] = jax.nn.silu(proj(h, base, HGRN_WIDTH)).astype(BF16)
        lb = lb_ref[...]
        f = lb + (1.0 - lb) * jax.nn.sigmoid(proj(h, base + HGRN_WIDTH, HGRN_WIDTH))
        hg_ref[rows, :] = jnp.log(f)
        hk_ref[rows, :] = (1.0 - f).astype(BF16)
        hv_ref[rows, :] = proj(h, base + 2 * HGRN_WIDTH, HGRN_WIDTH).astype(BF16)
        og_ref[rows, :] = jax.nn.silu(proj(h, base + 3 * HGRN_WIDTH, HGRN_WIDTH)).astype(BF16)

        base = ATTN_QKV + 4 * HGRN_WIDTH
        for c in range(GATE_WIDTH // GATE_CHUNK):
            cols = slice(c * GATE_CHUNK, (c + 1) * GATE_CHUNK)
            gate_ref[rows, cols] = jax.nn.sigmoid(proj(h, base + c * GATE_CHUNK, GATE_CHUNK)).astype(BF16)


def _in_proj(x2, nw, w_bf, lb, bsz, seq, tm):
    n = x2.shape[0]
    tps = seq // tm
    row = lambda width: pl.BlockSpec((tm, width), lambda i: (i, 0))
    qkv_specs = [pl.BlockSpec((None, d, tm // d, GROUP_QKV), lambda i: (i // tps, 0, i % tps, 0))
                 for d in ATTN_DILATIONS]
    qkv_shapes = [jax.ShapeDtypeStruct((bsz, d, seq // d, GROUP_QKV), BF16) for d in ATTN_DILATIONS]
    perms = jnp.asarray(np.stack([_residue_major_perm(IN_SUB, d) for d in ATTN_DILATIONS[1:]]), dtype=BF16)
    return pl.pallas_call(
        functools.partial(_in_proj_kernel, tm=tm),
        grid=(n // tm,),
        in_specs=[row(D_MODEL), _resident((1, D_MODEL)), _resident((D_MODEL, IN_WIDTH)),
                  _resident((1, HGRN_WIDTH)), _resident((N_GROUPS - 1, IN_SUB, IN_SUB))],
        out_specs=qkv_specs + [row(HGRN_WIDTH), row(SIDE_WIDTH)],
        out_shape=qkv_shapes + [jax.ShapeDtypeStruct((n, HGRN_WIDTH), F32),
                                jax.ShapeDtypeStruct((n, SIDE_WIDTH), BF16)],
        compiler_params=pltpu.CompilerParams(dimension_semantics=("arbitrary",),
                                             vmem_limit_bytes=VMEM_LIMIT),
        name="in_proj",
    )(x2, nw, w_bf, lb, perms)


def _bucket_table(dilation):
    blk = ATTN_BLOCK
    rel = np.arange(blk)[:, None] + blk - np.arange(2 * blk)[None, :]
    dist = np.maximum(rel * dilation, 0)
    nf = np.maximum(dist, 1).astype(np.float32)
    large = MAX_EXACT + (np.log(nf / np.float32(MAX_EXACT)) / np.float32(math.log(MAX_DISTANCE / MAX_EXACT))
                         * np.float32(NUM_BUCKETS - MAX_EXACT)).astype(np.int32)
    large = np.minimum(large, NUM_BUCKETS - 1)
    bucket = np.where(dist < MAX_EXACT, dist, large)
    in_win = (rel >= 0) & (rel <= blk)
    return np.where(in_win, bucket, -1).astype(np.int32)


def _bias_kernel(tab_ref, bucket_ref, bias_ref):
    h = pl.program_id(0)
    bucket = bucket_ref[0]
    acc = jnp.full(bucket.shape, NEG_INF, F32)
    for b in range(NUM_BUCKETS):
        acc = jnp.where(bucket == b, tab_ref[b, h] * LOG2E, acc)
    bias_ref[0, 0] = acc
    col = lax.broadcasted_iota(jnp.int32, bucket.shape, 1)
    bias_ref[1, 0] = jnp.where(col >= ATTN_BLOCK, acc, NEG_INF)


def _attn_bias(rel_bias):
    blk = ATTN_BLOCK
    n_heads = N_GROUPS * HEADS_PER_GROUP
    buckets = jnp.asarray(np.stack([_bucket_table(d) for d in ATTN_DILATIONS]))
    return pl.pallas_call(
        _bias_kernel,
        grid=(n_heads,),
        in_specs=[pl.BlockSpec(memory_space=pltpu.SMEM),
                  pl.BlockSpec((1, blk, 2 * blk), lambda h: (h // HEADS_PER_GROUP, 0, 0))],
        out_specs=pl.BlockSpec((2, 1, blk, 2 * blk), lambda h: (0, h, 0, 0)),
        out_shape=jax.ShapeDtypeStruct((2, n_heads, blk, 2 * blk), F32),
        compiler_params=pltpu.CompilerParams(dimension_semantics=("arbitrary",)),
        name="attn_bias",
    )(rel_bias, buckets)


ATTN_ROWS_PER_STEP = 2048


STAT_L_LANE = 64


def _attn_kernel(q_ref, kp_ref, kc_ref, vp_ref, vc_ref, bias_ref, o_ref, stat_ref, kbuf, vbuf, *, nres, rows):
    blk = ATTN_BLOCK
    pairs = HEADS_PER_GROUP // 2
    n = pl.program_id(2)
    ones = jnp.ones((blk + rows, LANES), BF16)
    for r in range(nres):
        kbuf[r, 0:blk] = kp_ref[r]
        kbuf[r, blk:blk + rows] = kc_ref[r]
        for pair in range(pairs):
            cols = slice(pair * LANES, (pair + 1) * LANES)
            vbuf[r, pair, 0:blk, 0:LANES] = vp_ref[r, :, cols]
            vbuf[r, pair, blk:blk + rows, 0:LANES] = vc_ref[r, :, cols]
            vbuf[r, pair, :, LANES:2 * LANES] = ones

    lane = lax.broadcasted_iota(jnp.int32, (blk, LANES), 1)
    low = lane < HEAD_DIM
    head_mask = (jnp.where(low, 1.0, 0.0).astype(BF16), jnp.where(low, 0.0, 1.0).astype(BF16))

    for r in range(nres):
        for i in range(rows // blk):
            variant = jnp.where(n == 0, 1, 0) if i == 0 else 0
            stat = jnp.where(lane < STAT_L_LANE, 0.0, 1.0)
            for pair in range(pairs):
                cols = slice(pair * LANES, (pair + 1) * LANES)
                q = q_ref[r, i * blk:(i + 1) * blk, cols]
                kk = kbuf[r, i * blk:(i + 2) * blk, cols]
                q2 = jnp.concatenate([q * head_mask[0], q * head_mask[1]], axis=0)
                s = lax.dot_general(q2, kk, NT_DIMS, preferred_element_type=F32)
                s = s + jnp.concatenate([bias_ref[variant, 2 * pair], bias_ref[variant, 2 * pair + 1]], axis=0)
                m = jnp.max(s, axis=-1, keepdims=True)
                p = jnp.exp2(s - m)
                pv = jnp.dot(p.astype(BF16), vbuf[r, pair, i * blk:(i + 2) * blk, :],
                             preferred_element_type=F32)
                o_ref[r, i * blk:(i + 1) * blk, cols] = jnp.where(
                    low, pv[0:blk, 0:LANES], pv[blk:2 * blk, 0:LANES]).astype(o_ref.dtype)
                for hh in range(2):
                    head = 2 * pair + hh
                    stat = jnp.where(lane == head, m[hh * blk:(hh + 1) * blk], stat)
                    stat = jnp.where(lane == STAT_L_LANE + head,
                                     pv[hh * blk:(hh + 1) * blk, LANES:2 * LANES], stat)
            stat_ref[r, i * blk:(i + 1) * blk, :] = stat


def _attention_group(qkv, bias, g):
    bsz, d, u, _ = qkv.shape
    blk = ATTN_BLOCK
    rows = min(ATTN_ROWS_PER_STEP, u)
    nres = min(ATTN_ROWS_PER_STEP // rows, d)
    nb = u // rows

    def cur(j):
        return pl.BlockSpec((None, nres, rows, ATTN_OUT), lambda b, r, n: (b, r, n, j))

    def prev(j):
        return pl.BlockSpec((None, nres, blk, ATTN_OUT),
                            lambda b, r, n: (b, r, jnp.maximum(n * (rows // blk) - 1, 0), j))

    stat = pl.BlockSpec((None, nres, rows, LANES), lambda b, r, n: (b, r, n, 0))
    stat_shape = jax.ShapeDtypeStruct((bsz, d, u, LANES), F32)
    return pl.pallas_call(
        functools.partial(_attn_kernel, nres=nres, rows=rows),
        grid=(bsz, d // nres, nb),
        in_specs=[cur(0), prev(1), cur(1), prev(2), cur(2),
                  pl.BlockSpec((2, HEADS_PER_GROUP, blk, 2 * blk), lambda b, r, n: (0, g, 0, 0),
                               pipeline_mode=pl.Buffered(1))],
        out_specs=[pl.BlockSpec((None, nres, rows, ATTN_OUT), lambda b, r, n: (b, r, n, 0)), stat],
        out_shape=[jax.ShapeDtypeStruct((bsz, d, u, ATTN_OUT), BF16), stat_shape],
        scratch_shapes=[pltpu.VMEM((nres, blk + rows, ATTN_OUT), BF16),
                        pltpu.VMEM((nres, HEADS_PER_GROUP // 2, blk + rows, 2 * LANES), BF16)],
        compiler_params=pltpu.CompilerParams(dimension_semantics=("arbitrary",) * 3,
                                             vmem_limit_bytes=VMEM_LIMIT),
        name=f"attn_g{g}",
    )(qkv, qkv, qkv, qkv, qkv, bias)


def _hgrn_chunks(q_ref, g_ref, k_ref, v_ref, og_ref, nw_ref, o_ref, st_ref):
    c_len, h_len = HGRN_CHUNK, HGRN_SUB
    r_i = lax.broadcasted_iota(jnp.int32, (c_len, c_len), 0)
    c_i = lax.broadcasted_iota(jnp.int32, (c_len, c_len), 1)
    tril = r_i >= c_i
    ones_tril = jnp.where(tril, 1.0, 0.0).astype(BF16)
    same_sub = jnp.logical_and(tril, c_i >= (r_i // h_len) * h_len)
    first_sub = lax.broadcasted_iota(jnp.int32, (c_len, HGRN_DK), 0) < h_len
    no_keys = jnp.zeros((h_len, HGRN_DK), BF16)
    nw = nw_ref[...]

    def chunk(c):
        sl = slice(c * c_len, (c + 1) * c_len)
        g = g_ref[sl, :]
        g1 = g.astype(BF16)
        g2 = (g - g1.astype(F32)).astype(BF16)
        cum_all = (jnp.dot(ones_tril, g1, preferred_element_type=F32)
                   + jnp.dot(ones_tril, g2, preferred_element_type=F32))
        for hd in range(HGRN_HEADS):
            cols = slice(hd * HGRN_DK, (hd + 1) * HGRN_DK)
            cum = cum_all[:, cols]
            mid0 = cum[h_len // 2 - 1:h_len // 2, :]
            mid1 = cum[h_len + h_len // 2 - 1:h_len + h_len // 2, :]
            edge = cum[h_len - 1:h_len, :]
            last = cum[c_len - 1:c_len, :]
            q = q_ref[sl, cols].astype(F32)
            k = k_ref[sl, cols].astype(F32)
            v = v_ref[sl, cols]
            rel = cum - jnp.where(first_sub, mid0, mid1)
            q_m = (q * jnp.exp(rel)).astype(BF16)
            k_m = (k * jnp.exp(-rel)).astype(BF16)
            a = jnp.where(same_sub, lax.dot_general(q_m, k_m, NT_DIMS, preferred_element_type=F32), 0.0)
            q_x = (q[h_len:] * jnp.exp(cum[h_len:] - edge)).astype(BF16)
            k_x = jnp.concatenate([(k[:h_len] * jnp.exp(edge - cum[:h_len])).astype(BF16), no_keys], axis=0)
            cross = lax.dot_general(q_x, k_x, NT_DIMS, preferred_element_type=F32)
            a = jnp.concatenate([a[:h_len], a[h_len:] + cross], axis=0).astype(BF16)
            state_t = st_ref[hd]
            q_d = (q * jnp.exp(cum)).astype(BF16)
            o = (jnp.dot(a, v, preferred_element_type=F32)
                 + lax.dot_general(q_d, state_t.astype(BF16), NT_DIMS, preferred_element_type=F32))
            k_d = (k * jnp.exp(last - cum)).astype(BF16)
            st_ref[hd] = state_t * jnp.exp(last) + lax.dot_general(v, k_d, TN_DIMS, preferred_element_type=F32)
            y = _rms(o, nw) * og_ref[sl, cols].astype(F32)
            o_ref[sl, cols] = y.astype(BF16)

    return chunk


MERGE_SUB = 256


def _mix_kernel(o1_ref, o2_ref, o3_ref, s1_ref, s2_ref, s3_ref, hg_ref, side_ref, hnw_ref,
                x_ref, expand_ref, wa_ref, wh_ref, wo_ref, nw_ref, out_ref,
                o_tok, s_tok, y_buf, merged_buf, yh_ref, st_ref, *, tm, tiles_per_seq):
    hq_ref, hk_ref, hv_ref, og_ref, gate_ref = _side_views(side_ref)

    @pl.when(pl.program_id(0) % tiles_per_seq == 0)
    def _():
        st_ref[...] = jnp.zeros_like(st_ref)

    hgrn_chunk = _hgrn_chunks(hq_ref, hg_ref, hk_ref, hv_ref, og_ref, hnw_ref, yh_ref, st_ref)

    slabs = ATTN_OUT // LANES
    sub = MERGE_SUB
    chunks_per_sub = sub // HGRN_CHUNK
    lane = lax.broadcasted_iota(jnp.int32, (sub, LANES), 1)
    copy_lanes = (lane % STAT_L_LANE) // HEADS_PER_GROUP == 1
    expand = expand_ref[...]

    def merge_weights(t):
        rows = slice(t * sub, (t + 1) * sub)
        os = [o1_ref[0, rows, :].astype(F32)]
        stats = [s1_ref[0, rows, :]]
        for g, (o_ref, s_ref) in enumerate(((o2_ref, s2_ref), (o3_ref, s3_ref))):
            d = ATTN_DILATIONS[g + 1]
            per = sub // d
            for r in range(d):
                tok_rows = pl.ds(r, per, stride=d)
                s_tok[t, g, tok_rows, :] = s_ref[r, t * per:(t + 1) * per, :]
                for s in range(slabs):
                    o_tok[t, g, s, tok_rows, :] = (
                        o_ref[r, t * per:(t + 1) * per, s * LANES:(s + 1) * LANES].astype(F32))
            os.append(jnp.concatenate([o_tok[t, g, s] for s in range(slabs)], axis=1))
            stats.append(s_tok[t, g])
        stats = [jnp.where(copy_lanes, pltpu.roll(st, HEADS_PER_GROUP, axis=1), st) for st in stats]
        mx = jnp.maximum(jnp.maximum(stats[0], stats[1]), stats[2])
        es = [jnp.exp2(st - mx) for st in stats]
        ls = [pltpu.roll(st, STAT_L_LANE, axis=1) for st in stats]
        den = ls[0] * es[0] + ls[1] * es[1] + ls[2] * es[2]
        y = None
        for e, o in zip(es, os):
            w = e / den
            w_hi = w.astype(BF16).astype(F32)
            w_split = jnp.where(lane < HEADS_PER_GROUP, w_hi, w - w_hi)
            w_split = jnp.where(lane < 2 * HEADS_PER_GROUP, w_split, 0.0).astype(BF16)
            term = jnp.dot(w_split, expand, preferred_element_type=F32) * o
            y = term if y is None else y + term
        y_buf[rows, :] = y.astype(BF16)

    def merge_project(t):
        rows = slice(t * sub, (t + 1) * sub)
        pa = jnp.dot(y_buf[rows, :], wa_ref[...], preferred_element_type=F32)
        ph = jnp.dot(yh_ref[rows, :], wh_ref[...], preferred_element_type=F32)
        merged_buf[rows, :] = (gate_ref[rows, 0:D_MODEL].astype(F32) * pa
                               + gate_ref[rows, D_MODEL:GATE_WIDTH].astype(F32) * ph).astype(BF16)
        z = jnp.dot(merged_buf[rows, :], wo_ref[...], preferred_element_type=F32)
        out_ref[rows, :] = x_ref[rows, :] + _rms(z, nw_ref[...])

    for t in range(tm // sub):
        for c in range(chunks_per_sub):
            hgrn_chunk(t * chunks_per_sub + c)
            if c == 0:
                merge_weights(t)
        merge_project(t)


def _mix(os, stats, hg, side, hnw, x2, wa, wh, wo, nw, seq, tm):
    n = x2.shape[0]
    tps = seq // tm
    row = lambda width: pl.BlockSpec((tm, width), lambda i: (i, 0))
    grouped = lambda width: [pl.BlockSpec((None, d, tm // d, width), lambda i: (i // tps, 0, i % tps, 0))
                             for d in ATTN_DILATIONS]
    head_of_col = np.arange(ATTN_OUT) // HEAD_DIM
    head_of_lane = np.where(np.arange(LANES) < 2 * HEADS_PER_GROUP, np.arange(LANES) % HEADS_PER_GROUP, -1)
    expand = jnp.asarray(head_of_lane[:, None] == head_of_col[None, :], dtype=BF16)
    return pl.pallas_call(
        functools.partial(_mix_kernel, tm=tm, tiles_per_seq=tps),
        grid=(n // tm,),
        in_specs=grouped(ATTN_OUT) + grouped(LANES) + [row(HGRN_WIDTH), row(SIDE_WIDTH),
                  _resident((1, HGRN_DK)), row(D_MODEL),
                  _resident((LANES, ATTN_OUT)), _resident((ATTN_OUT, D_MODEL)),
                  _resident((HGRN_WIDTH, D_MODEL)), _resident((D_MODEL, D_MODEL)), _resident((1, D_MODEL))],
        out_specs=row(D_MODEL),
        out_shape=jax.ShapeDtypeStruct((n, D_MODEL), F32),
        scratch_shapes=[pltpu.VMEM((tm // MERGE_SUB, N_GROUPS - 1, ATTN_OUT // LANES, MERGE_SUB, LANES), F32),
                        pltpu.VMEM((tm // MERGE_SUB, N_GROUPS - 1, MERGE_SUB, LANES), F32),
                        pltpu.VMEM((tm, ATTN_OUT), BF16),
                        pltpu.VMEM((tm, D_MODEL), BF16),
                        pltpu.VMEM((tm, HGRN_WIDTH), BF16),
                        pltpu.VMEM((HGRN_HEADS, HGRN_DK, HGRN_DK), F32)],
        compiler_params=pltpu.CompilerParams(dimension_semantics=("arbitrary",),
                                             vmem_limit_bytes=VMEM_LIMIT),
        name="mix_out",
    )(*os, *stats, hg, side, hnw, x2, expand, wa, wh, wo, nw)


FFN_CHUNK = 256
FFN_SUB = 512
DOWN_PIECE = 256
CONV_ROWS = 64
HALO = SUBLANES
UBUF_SLOTS = 4


def _ffn_kernel(x_ref, nw1_ref, wup_ref, cw_ref, cb_ref, wdn_ref, nw2_ref, out_ref,
                ubuf, carry, act, hbuf, *, tm, tiles_per_seq):
    sub = FFN_SUB
    n_sub = tm // sub
    n_chunks = D_FF // FFN_CHUNK
    n_pieces = D_MODEL // DOWN_PIECE
    first = (pl.program_id(0) % tiles_per_seq) == 0

    @pl.when(pl.program_id(0) == 0)
    def _():
        carry[...] = jnp.zeros_like(carry)

    slabs = FFN_CHUNK // LANES

    def up_chunk(s, c):
        for half in range(2):
            col0 = half * D_FF + c * FFN_CHUNK
            u = jnp.dot(hbuf[s], wup_ref[:, col0:col0 + FFN_CHUNK], preferred_element_type=F32)
            for sl in range(slabs):
                slot = (2 * c + half) * slabs + sl
                buf = ubuf.at[c % UBUF_SLOTS, half, sl]
                prev = carry[slot]
                buf[0:HALO, :] = jnp.where(first, 0.0, prev) if s == 0 else prev
                buf[HALO:HALO + sub, :] = u[:, sl * LANES:(sl + 1) * LANES]
                carry[slot] = buf[sub:HALO + sub, :]
        for sl in range(slabs):
            for rb in range(sub // CONV_ROWS):
                conv = []
                for half in range(2):
                    col0 = half * D_FF + c * FFN_CHUNK + sl * LANES
                    buf = ubuf.at[c % UBUF_SLOTS, half, sl]
                    w = cw_ref[:, col0:col0 + LANES]
                    r0 = HALO + rb * CONV_ROWS
                    conv.append(cb_ref[:, col0:col0 + LANES]
                                + w[0:1] * buf[r0 - 2:r0 - 2 + CONV_ROWS, :]
                                + w[1:2] * buf[r0 - 1:r0 - 1 + CONV_ROWS, :]
                                + w[2:3] * buf[r0:r0 + CONV_ROWS, :])
                gate, val = conv
                gelu = 0.5 * gate * (1.0 + lax.erf(gate * (2.0 ** -0.5)))
                act[s, rb * CONV_ROWS:(rb + 1) * CONV_ROWS,
                    c * FFN_CHUNK + sl * LANES:c * FFN_CHUNK + (sl + 1) * LANES] = (gelu * val).astype(BF16)

    piece_after_chunk = {(n_chunks * (p + 1)) // n_pieces - 1: p for p in range(n_pieces)}

    for s in range(n_sub + 1):
        if s < n_sub:
            hbuf[s] = _rms(x_ref[s * sub:(s + 1) * sub, :], nw1_ref[...]).astype(BF16)
        pieces = []
        for c in range(n_chunks):
            if s < n_sub:
                up_chunk(s, c)
            if s > 0 and c in piece_after_chunk:
                p = piece_after_chunk[c]
                pieces.append(jnp.dot(act[s - 1], wdn_ref[:, p * DOWN_PIECE:(p + 1) * DOWN_PIECE],
                                      preferred_element_type=F32))
        if s > 0:
            rows = slice((s - 1) * sub, s * sub)
            y = jnp.concatenate(pieces, axis=1)
            out_ref[rows, :] = x_ref[rows, :] + _rms(y, nw2_ref[...])


def _ffn(x2, nw1, wup, cw, cb, wdn, nw2, seq, tm):
    n = x2.shape[0]
    row = pl.BlockSpec((tm, D_MODEL), lambda i: (i, 0))
    return pl.pallas_call(
        functools.partial(_ffn_kernel, tm=tm, tiles_per_seq=seq // tm),
        grid=(n // tm,),
        in_specs=[row, _resident((1, D_MODEL)), _resident((D_MODEL, 2 * D_FF)),
                  _resident((CONV_WIDTH, 2 * D_FF)), _resident((1, 2 * D_FF)),
                  _resident((D_FF, D_MODEL)), _resident((1, D_MODEL))],
        out_specs=row,
        out_shape=jax.ShapeDtypeStruct((n, D_MODEL), F32),
        scratch_shapes=[pltpu.VMEM((UBUF_SLOTS, 2, FFN_CHUNK // LANES, HALO + FFN_SUB, LANES), F32),
                        pltpu.VMEM((2 * D_FF // LANES, HALO, LANES), F32),
                        pltpu.VMEM((tm // FFN_SUB, FFN_SUB, D_FF), BF16),
                        pltpu.VMEM((tm // FFN_SUB, FFN_SUB, D_MODEL), BF16)],
        compiler_params=pltpu.CompilerParams(dimension_semantics=("arbitrary",),
                                             vmem_limit_bytes=VMEM_LIMIT),
        name="conv_ffn",
    )(x2, nw1, wup, cw, cb, wdn, nw2)


def kernel(x, pre_mix_norm, w_in, rel_bias, hgrn_lb_raw, hgrn_norm, w_branch_attn, w_branch_hgrn,
           w_out, post_mix_norm, pre_ffn_norm, w_up, conv_w, conv_b, w_down, post_ffn_norm):
    bsz, seq, d_model = x.shape
    depth = w_in.shape[0]
    n = bsz * seq
    assert d_model == D_MODEL and w_in.shape[1:] == (D_MODEL, IN_WIDTH) and w_up.shape[1:] == (D_MODEL, 2 * D_FF)
    assert rel_bias.shape == (NUM_BUCKETS, N_GROUPS * HEADS_PER_GROUP)
    assert seq % (max(ATTN_DILATIONS) * ATTN_BLOCK) == 0 and seq % ROW_TILE == 0
    lbs = jnp.cumsum(jax.nn.softmax(hgrn_lb_raw.astype(F32), axis=0), axis=0)
    x2 = x.reshape(n, D_MODEL)
    bias = _attn_bias(rel_bias)
    for l in range(depth):
        *qkvs, hg, side = _in_proj(
            x2, pre_mix_norm[l][None], w_in[l].astype(BF16), lbs[l][None], bsz, seq, tm=ROW_TILE)
        os, stats = [], []
        for g in range(N_GROUPS):
            o, stat = _attention_group(qkvs[g], bias, g)
            os.append(o)
            stats.append(stat)
        x2 = _mix(os, stats, hg, side, hgrn_norm[l][None], x2,
                  w_branch_attn[l].astype(BF16), w_branch_hgrn[l].astype(BF16),
                  w_out[l].astype(BF16), post_mix_norm[l][None], seq, tm=ROW_TILE)
        x2 = _ffn(x2, pre_ffn_norm[l][None], w_up[l].astype(BF16), conv_w[l], conv_b[l][None],
                  w_down[l].astype(BF16), post_ffn_norm[l][None], seq, tm=ROW_TILE)
    return x2.reshape(bsz, seq, D_MODEL)
```

```python
import functools
import math

import numpy as np
import jax
import jax.numpy as jnp
from jax import lax
from jax.experimental import pallas as pl
from jax.experimental.pallas import tpu as pltpu

F32 = jnp.float32
BF16 = jnp.bfloat16

D_MODEL = 1024
ATTN_WINDOWS = (128, 512, 2048)
ATTN_DILATIONS = (1, 4, 16)
N_GROUPS = 3
HEAD_DIM = 64
HEADS_PER_GROUP = 8
ATTN_OUT = HEADS_PER_GROUP * HEAD_DIM
GROUP_QKV = 3 * ATTN_OUT
ATTN_QKV = N_GROUPS * GROUP_QKV
ATTN_BLOCK = 128
NEG_INF = -1e30
NUM_BUCKETS = 32
MAX_EXACT = 16
MAX_DISTANCE = 2048
HGRN_HEADS = 4
HGRN_DK = 128
HGRN_WIDTH = HGRN_HEADS * HGRN_DK
HGRN_CHUNK = 128
HGRN_SUB = 64
GATE_WIDTH = 2 * D_MODEL
SIDE_WIDTH = 4 * HGRN_WIDTH
GATE_COL0 = ATTN_QKV + 4 * HGRN_WIDTH
IN_WIDTH = ATTN_QKV + 4 * HGRN_WIDTH + GATE_WIDTH
D_FF = 2816
CONV_WIDTH = 3
NORM_EPS = 1e-6
LOG2E = math.log2(math.e)

assert all(w // d == ATTN_BLOCK for w, d in zip(ATTN_WINDOWS, ATTN_DILATIONS))

LANES = 128
SUBLANES = 8
V7X_VMEM_BYTES = 64 * 1024 * 1024
VMEM_LIMIT = V7X_VMEM_BYTES * 7 // 8
ROW_TILE = 512
IN_ROW_TILE = 1024

NT_DIMS = (((1,), (1,)), ((), ()))
TN_DIMS = (((0,), (0,)), ((), ()))


def _rms(xf, w):
    return xf * lax.rsqrt(jnp.mean(xf * xf, axis=-1, keepdims=True) + NORM_EPS) * w


def _resident(shape):
    nd = len(shape)
    return pl.BlockSpec(shape, lambda *_: (0,) * nd, pipeline_mode=pl.Buffered(1))


IN_SUB = 256
GATE_CHUNK = 512


def _residue_major_perm(rows, d):
    t = np.arange(rows)
    p = np.zeros((rows, rows), np.float32)
    p[(t % d) * (rows // d) + t // d, t] = 1.0
    return p


def _side_views(side_ref):
    w = HGRN_WIDTH
    return (side_ref.at[:, 0:w], side_ref.at[:, w:2 * w], side_ref.at[:, 2 * w:3 * w],
            side_ref.at[:, 3 * w:SIDE_WIDTH])


def _in_proj_kernel(x_ref, nw_ref, w_ref, lb_ref, pm_ref,
                    qkv0_ref, qkv1_ref, qkv2_ref, hg_ref, side_ref, *, tm):
    hq_ref, hk_ref, hv_ref, og_ref = _side_views(side_ref)
    sub = IN_SUB
    for t in range(tm // sub):
        rows = slice(t * sub, (t + 1) * sub)
        h = _rms(x_ref[rows, :], nw_ref[...]).astype(BF16)

        def proj(lhs, c0, width):
            return jnp.dot(lhs, w_ref[:, c0:c0 + width], preferred_element_type=F32)

        for g, out_ref in enumerate((qkv0_ref, qkv1_ref, qkv2_ref)):
            d = ATTN_DILATIONS[g]
            h_g = h if d == 1 else jnp.dot(pm_ref[g - 1], h, preferred_element_type=F32).astype(BF16)
            per = sub // d
            for j in range(3):
                cols = slice(j * ATTN_OUT, (j + 1) * ATTN_OUT)
                acc = proj(h_g, (3 * g + j) * ATTN_OUT, ATTN_OUT)
                if j == 0:
                    acc = acc * (HEAD_DIM ** -0.5 * LOG2E)
                acc = acc.astype(BF16)
                for r in range(d):
                    out_ref[r, t * per:(t + 1) * per, cols] = acc[r * per:(r + 1) * per, :]

        base = ATTN_QKV
        hq_ref[rows, :] = jax.nn.silu(proj(h, base, HGRN_WIDTH)).astype(BF16)
        lb = lb_ref[...]
        f = lb + (1.0 - lb) * jax.nn.sigmoid(proj(h, base + HGRN_WIDTH, HGRN_WIDTH))
        hg_ref[rows, :] = jnp.log(f)
        hk_ref[rows, :] = (1.0 - f).astype(BF16)
        hv_ref[rows, :] = proj(h, base + 2 * HGRN_WIDTH, HGRN_WIDTH).astype(BF16)
        og_ref[rows, :] = jax.nn.silu(proj(h, base + 3 * HGRN_WIDTH, HGRN_WIDTH)).astype(BF16)


def _in_proj(x2, nw, w_bf, lb, bsz, seq, tm):
    n = x2.shape[0]
    tps = seq // tm
    row = lambda width: pl.BlockSpec((tm, width), lambda i: (i, 0))
    qkv_specs = [pl.BlockSpec((None, d, tm // d, GROUP_QKV), lambda i: (i // tps, 0, i % tps, 0))
                 for d in ATTN_DILATIONS]
    qkv_shapes = [jax.ShapeDtypeStruct((bsz, d, seq // d, GROUP_QKV), BF16) for d in ATTN_DILATIONS]
    perms = jnp.asarray(np.stack([_residue_major_perm(IN_SUB, d) for d in ATTN_DILATIONS[1:]]), dtype=BF16)
    return pl.pallas_call(
        functools.partial(_in_proj_kernel, tm=tm),
        grid=(n // tm,),
        in_specs=[row(D_MODEL), _resident((1, D_MODEL)), _resident((D_MODEL, GATE_COL0)),
                  _resident((1, HGRN_WIDTH)), _resident((N_GROUPS - 1, IN_SUB, IN_SUB))],
        out_specs=qkv_specs + [row(HGRN_WIDTH), row(SIDE_WIDTH)],
        out_shape=qkv_shapes + [jax.ShapeDtypeStruct((n, HGRN_WIDTH), F32),
                                jax.ShapeDtypeStruct((n, SIDE_WIDTH), BF16)],
        compiler_params=pltpu.CompilerParams(dimension_semantics=("arbitrary",),
                                             vmem_limit_bytes=VMEM_LIMIT),
        name="in_proj",
    )(x2, nw, w_bf, lb, perms)


def _bucket_table(dilation):
    blk = ATTN_BLOCK
    rel = np.arange(blk)[:, None] + blk - np.arange(2 * blk)[None, :]
    dist = np.maximum(rel * dilation, 0)
    nf = np.maximum(dist, 1).astype(np.float32)
    large = MAX_EXACT + (np.log(nf / np.float32(MAX_EXACT)) / np.float32(math.log(MAX_DISTANCE / MAX_EXACT))
                         * np.float32(NUM_BUCKETS - MAX_EXACT)).astype(np.int32)
    large = np.minimum(large, NUM_BUCKETS - 1)
    bucket = np.where(dist < MAX_EXACT, dist, large)
    in_win = (rel >= 0) & (rel <= blk)
    return np.where(in_win, bucket, -1).astype(np.int32)


def _bias_kernel(tab_ref, bucket_ref, bias_ref):
    h = pl.program_id(0)
    bucket = bucket_ref[0]
    acc = jnp.full(bucket.shape, NEG_INF, F32)
    for b in range(NUM_BUCKETS):
        acc = jnp.where(bucket == b, tab_ref[b, h] * LOG2E, acc)
    bias_ref[0, 0] = acc
    col = lax.broadcasted_iota(jnp.int32, bucket.shape, 1)
    bias_ref[1, 0] = jnp.where(col >= ATTN_BLOCK, acc, NEG_INF)


def _attn_bias(rel_bias):
    blk = ATTN_BLOCK
    n_heads = N_GROUPS * HEADS_PER_GROUP
    buckets = jnp.asarray(np.stack([_bucket_table(d) for d in ATTN_DILATIONS]))
    return pl.pallas_call(
        _bias_kernel,
        grid=(n_heads,),
        in_specs=[pl.BlockSpec(memory_space=pltpu.SMEM),
                  pl.BlockSpec((1, blk, 2 * blk), lambda h: (h // HEADS_PER_GROUP, 0, 0))],
        out_specs=pl.BlockSpec((2, 1, blk, 2 * blk), lambda h: (0, h, 0, 0)),
        out_shape=jax.ShapeDtypeStruct((2, n_heads, blk, 2 * blk), F32),
        compiler_params=pltpu.CompilerParams(dimension_semantics=("arbitrary",)),
        name="attn_bias",
    )(rel_bias, buckets)


ATTN_ROWS_PER_STEP = 2048


STAT_L_LANE = 64


def _attn_kernel(q_ref, kp_ref, kc_ref, vp_ref, vc_ref, bias_ref, o_ref, stat_ref, kbuf, vbuf, *, nres, rows):
    blk = ATTN_BLOCK
    pairs = HEADS_PER_GROUP // 2
    n = pl.program_id(2)
    ones = jnp.ones((blk + rows, LANES), BF16)
    for r in range(nres):
        kbuf[r, 0:blk] = kp_ref[r]
        kbuf[r, blk:blk + rows] = kc_ref[r]
        for pair in range(pairs):
            cols = slice(pair * LANES, (pair + 1) * LANES)
            vbuf[r, pair, 0:blk, 0:LANES] = vp_ref[r, :, cols]
            vbuf[r, pair, blk:blk + rows, 0:LANES] = vc_ref[r, :, cols]
            vbuf[r, pair, :, LANES:2 * LANES] = ones

    lane = lax.broadcasted_iota(jnp.int32, (blk, LANES), 1)
    low = lane < HEAD_DIM
    head_mask = (jnp.where(low, 1.0, 0.0).astype(BF16), jnp.where(low, 0.0, 1.0).astype(BF16))

    for r in range(nres):
        for i in range(rows // blk):
            variant = jnp.where(n == 0, 1, 0) if i == 0 else 0
            stat = jnp.where(lane < STAT_L_LANE, 0.0, 1.0)
            for pair in range(pairs):
                cols = slice(pair * LANES, (pair + 1) * LANES)
                q = q_ref[r, i * blk:(i + 1) * blk, cols]
                kk = kbuf[r, i * blk:(i + 2) * blk, cols]
                q2 = jnp.concatenate([q * head_mask[0], q * head_mask[1]], axis=0)
                s = lax.dot_general(q2, kk, NT_DIMS, preferred_element_type=F32)
                s = s + jnp.concatenate([bias_ref[variant, 2 * pair], bias_ref[variant, 2 * pair + 1]], axis=0)
                m = jnp.max(s, axis=-1, keepdims=True)
                p = jnp.exp2(s - m)
                pv = jnp.dot(p.astype(BF16), vbuf[r, pair, i * blk:(i + 2) * blk, :],
                             preferred_element_type=F32)
                o_ref[r, i * blk:(i + 1) * blk, cols] = jnp.where(
                    low, pv[0:blk, 0:LANES], pv[blk:2 * blk, 0:LANES]).astype(o_ref.dtype)
                for hh in range(2):
                    head = 2 * pair + hh
                    stat = jnp.where(lane == head, m[hh * blk:(hh + 1) * blk], stat)
                    stat = jnp.where(lane == STAT_L_LANE + head,
                                     pv[hh * blk:(hh + 1) * blk, LANES:2 * LANES], stat)
            stat_ref[r, i * blk:(i + 1) * blk, :] = stat


def _attention_group(qkv, bias, g):
    bsz, d, u, _ = qkv.shape
    blk = ATTN_BLOCK
    rows = min(ATTN_ROWS_PER_STEP, u)
    nres = min(ATTN_ROWS_PER_STEP // rows, d)
    nb = u // rows

    def cur(j):
        return pl.BlockSpec((None, nres, rows, ATTN_OUT), lambda b, r, n: (b, r, n, j))

    def prev(j):
        return pl.BlockSpec((None, nres, blk, ATTN_OUT),
                            lambda b, r, n: (b, r, jnp.maximum(n * (rows // blk) - 1, 0), j))

    stat = pl.BlockSpec((None, nres, rows, LANES), lambda b, r, n: (b, r, n, 0))
    stat_shape = jax.ShapeDtypeStruct((bsz, d, u, LANES), F32)
    return pl.pallas_call(
        functools.partial(_attn_kernel, nres=nres, rows=rows),
        grid=(bsz, d // nres, nb),
        in_specs=[cur(0), prev(1), cur(1), prev(2), cur(2),
                  pl.BlockSpec((2, HEADS_PER_GROUP, blk, 2 * blk), lambda b, r, n: (0, g, 0, 0),
                               pipeline_mode=pl.Buffered(1))],
        out_specs=[pl.BlockSpec((None, nres, rows, ATTN_OUT), lambda b, r, n: (b, r, n, 0)), stat],
        out_shape=[jax.ShapeDtypeStruct((bsz, d, u, ATTN_OUT), BF16), stat_shape],
        scratch_shapes=[pltpu.VMEM((nres, blk + rows, ATTN_OUT), BF16),
                        pltpu.VMEM((nres, HEADS_PER_GROUP // 2, blk + rows, 2 * LANES), BF16)],
        compiler_params=pltpu.CompilerParams(dimension_semantics=("arbitrary",) * 3,
                                             vmem_limit_bytes=VMEM_LIMIT),
        name=f"attn_g{g}",
    )(qkv, qkv, qkv, qkv, qkv, bias)


def _hgrn_chunks(q_ref, g_ref, k_ref, v_ref, og_ref, nw_ref, o_ref, st_ref):
    c_len, h_len = HGRN_CHUNK, HGRN_SUB
    r_i = lax.broadcasted_iota(jnp.int32, (c_len, c_len), 0)
    c_i = lax.broadcasted_iota(jnp.int32, (c_len, c_len), 1)
    tril = r_i >= c_i
    ones_tril = jnp.where(tril, 1.0, 0.0).astype(BF16)
    same_sub = jnp.logical_and(tril, c_i >= (r_i // h_len) * h_len)
    first_sub = lax.broadcasted_iota(jnp.int32, (c_len, HGRN_DK), 0) < h_len
    no_keys = jnp.zeros((h_len, HGRN_DK), BF16)
    nw = nw_ref[...]

    def chunk(c):
        sl = slice(c * c_len, (c + 1) * c_len)
        g = g_ref[sl, :]
        g1 = g.astype(BF16)
        g2 = (g - g1.astype(F32)).astype(BF16)
        cum_all = (jnp.dot(ones_tril, g1, preferred_element_type=F32)
                   + jnp.dot(ones_tril, g2, preferred_element_type=F32))
        for hd in range(HGRN_HEADS):
            cols = slice(hd * HGRN_DK, (hd + 1) * HGRN_DK)
            cum = cum_all[:, cols]
            mid0 = cum[h_len // 2 - 1:h_len // 2, :]
            mid1 = cum[h_len + h_len // 2 - 1:h_len + h_len // 2, :]
            edge = cum[h_len - 1:h_len, :]
            last = cum[c_len - 1:c_len, :]
            q = q_ref[sl, cols].astype(F32)
            k = k_ref[sl, cols].astype(F32)
            v = v_ref[sl, cols]
            rel = cum - jnp.where(first_sub, mid0, mid1)
            q_m = (q * jnp.exp(rel)).astype(BF16)
            k_m = (k * jnp.exp(-rel)).astype(BF16)
            a = jnp.where(same_sub, lax.dot_general(q_m, k_m, NT_DIMS, preferred_element_type=F32), 0.0)
            q_x = (q[h_len:] * jnp.exp(cum[h_len:] - edge)).astype(BF16)
            k_x = jnp.concatenate([(k[:h_len] * jnp.exp(edge - cum[:h_len])).astype(BF16), no_keys], axis=0)
            cross = lax.dot_general(q_x, k_x, NT_DIMS, preferred_element_type=F32)
            a = jnp.concatenate([a[:h_len], a[h_len:] + cross], axis=0).astype(BF16)
            state_t = st_ref[hd]
            q_d = (q * jnp.exp(cum)).astype(BF16)
            o = (jnp.dot(a, v, preferred_element_type=F32)
                 + lax.dot_general(q_d, state_t.astype(BF16), NT_DIMS, preferred_element_type=F32))
            k_d = (k * jnp.exp(last - cum)).astype(BF16)
            st_ref[hd] = state_t * jnp.exp(last) + lax.dot_general(v, k_d, TN_DIMS, preferred_element_type=F32)
            y = _rms(o, nw) * og_ref[sl, cols].astype(F32)
            o_ref[sl, cols] = y.astype(BF16)

    return chunk


MERGE_SUB = 256


def _mix_kernel(o1_ref, o2_ref, o3_ref, s1_ref, s2_ref, s3_ref, hg_ref, side_ref, hnw_ref,
                x_ref, pre_nw_ref, wg_ref, expand_ref, wa_ref, wh_ref, wo_ref, nw_ref, out_ref,
                o_tok, s_tok, y_buf, merged_buf, yh_ref, st_ref, h_buf, *, tm, tiles_per_seq):
    hq_ref, hk_ref, hv_ref, og_ref = _side_views(side_ref)

    @pl.when(pl.program_id(0) % tiles_per_seq == 0)
    def _():
        st_ref[...] = jnp.zeros_like(st_ref)

    hgrn_chunk = _hgrn_chunks(hq_ref, hg_ref, hk_ref, hv_ref, og_ref, hnw_ref, yh_ref, st_ref)

    slabs = ATTN_OUT // LANES
    sub = MERGE_SUB
    chunks_per_sub = sub // HGRN_CHUNK
    lane = lax.broadcasted_iota(jnp.int32, (sub, LANES), 1)
    copy_lanes = (lane % STAT_L_LANE) // HEADS_PER_GROUP == 1
    expand = expand_ref[...]

    def merge_weights(t):
        rows = slice(t * sub, (t + 1) * sub)
        os = [o1_ref[0, rows, :].astype(F32)]
        stats = [s1_ref[0, rows, :]]
        for g, (o_ref, s_ref) in enumerate(((o2_ref, s2_ref), (o3_ref, s3_ref))):
            d = ATTN_DILATIONS[g + 1]
            per = sub // d
            for r in range(d):
                tok_rows = pl.ds(r, per, stride=d)
                s_tok[t, g, tok_rows, :] = s_ref[r, t * per:(t + 1) * per, :]
                for s in range(slabs):
                    o_tok[t, g, s, tok_rows, :] = (
                        o_ref[r, t * per:(t + 1) * per, s * LANES:(s + 1) * LANES].astype(F32))
            os.append(jnp.concatenate([o_tok[t, g, s] for s in range(slabs)], axis=1))
            stats.append(s_tok[t, g])
        stats = [jnp.where(copy_lanes, pltpu.roll(st, HEADS_PER_GROUP, axis=1), st) for st in stats]
        mx = jnp.maximum(jnp.maximum(stats[0], stats[1]), stats[2])
        es = [jnp.exp2(st - mx) for st in stats]
        ls = [pltpu.roll(st, STAT_L_LANE, axis=1) for st in stats]
        den = ls[0] * es[0] + ls[1] * es[1] + ls[2] * es[2]
        y = None
        for e, o in zip(es, os):
            w = e / den
            w_hi = w.astype(BF16).astype(F32)
            w_split = jnp.where(lane < HEADS_PER_GROUP, w_hi, w - w_hi)
            w_split = jnp.where(lane < 2 * HEADS_PER_GROUP, w_split, 0.0).astype(BF16)
            term = jnp.dot(w_split, expand, preferred_element_type=F32) * o
            y = term if y is None else y + term
        y_buf[rows, :] = y.astype(BF16)

    def merge_project(t):
        rows = slice(t * sub, (t + 1) * sub)
        h_buf[rows, :] = _rms(x_ref[rows, :], pre_nw_ref[...]).astype(BF16)
        for c in range(D_MODEL // GATE_CHUNK):
            cols = slice(c * GATE_CHUNK, (c + 1) * GATE_CHUNK)
            gate_a = jax.nn.sigmoid(jnp.dot(h_buf[rows, :], wg_ref[:, cols], preferred_element_type=F32))
            gate_h = jax.nn.sigmoid(jnp.dot(h_buf[rows, :], wg_ref[:, D_MODEL + c * GATE_CHUNK:
                                                                 D_MODEL + (c + 1) * GATE_CHUNK],
                                            preferred_element_type=F32))
            pa = jnp.dot(y_buf[rows, :], wa_ref[:, cols], preferred_element_type=F32)
            ph = jnp.dot(yh_ref[rows, :], wh_ref[:, cols], preferred_element_type=F32)
            merged_buf[rows, cols] = (gate_a * pa + gate_h * ph).astype(BF16)
        z = jnp.dot(merged_buf[rows, :], wo_ref[...], preferred_element_type=F32)
        out_ref[rows, :] = x_ref[rows, :] + _rms(z, nw_ref[...])

    for t in range(tm // sub):
        for c in range(chunks_per_sub):
            hgrn_chunk(t * chunks_per_sub + c)
            if c == 0:
                merge_weights(t)
        merge_project(t)


def _mix(os, stats, hg, side, hnw, x2, pre_nw, wg, wa, wh, wo, nw, seq, tm):
    n = x2.shape[0]
    tps = seq // tm
    row = lambda width: pl.BlockSpec((tm, width), lambda i: (i, 0))
    grouped = lambda width: [pl.BlockSpec((None, d, tm // d, width), lambda i: (i // tps, 0, i % tps, 0))
                             for d in ATTN_DILATIONS]
    head_of_col = np.arange(ATTN_OUT) // HEAD_DIM
    head_of_lane = np.where(np.arange(LANES) < 2 * HEADS_PER_GROUP, np.arange(LANES) % HEADS_PER_GROUP, -1)
    expand = jnp.asarray(head_of_lane[:, None] == head_of_col[None, :], dtype=BF16)
    return pl.pallas_call(
        functools.partial(_mix_kernel, tm=tm, tiles_per_seq=tps),
        grid=(n // tm,),
        in_specs=grouped(ATTN_OUT) + grouped(LANES) + [row(HGRN_WIDTH), row(SIDE_WIDTH),
                  _resident((1, HGRN_DK)), row(D_MODEL), _resident((1, D_MODEL)),
                  _resident((D_MODEL, GATE_WIDTH)),
                  _resident((LANES, ATTN_OUT)), _resident((ATTN_OUT, D_MODEL)),
                  _resident((HGRN_WIDTH, D_MODEL)), _resident((D_MODEL, D_MODEL)), _resident((1, D_MODEL))],
        out_specs=row(D_MODEL),
        out_shape=jax.ShapeDtypeStruct((n, D_MODEL), F32),
        scratch_shapes=[pltpu.VMEM((tm // MERGE_SUB, N_GROUPS - 1, ATTN_OUT // LANES, MERGE_SUB, LANES), F32),
                        pltpu.VMEM((tm // MERGE_SUB, N_GROUPS - 1, MERGE_SUB, LANES), F32),
                        pltpu.VMEM((tm, ATTN_OUT), BF16),
                        pltpu.VMEM((tm, D_MODEL), BF16),
                        pltpu.VMEM((tm, HGRN_WIDTH), BF16),
                        pltpu.VMEM((HGRN_HEADS, HGRN_DK, HGRN_DK), F32),
                        pltpu.VMEM((tm, D_MODEL), BF16)],
        compiler_params=pltpu.CompilerParams(dimension_semantics=("arbitrary",),
                                             vmem_limit_bytes=VMEM_LIMIT),
        name="mix_out",
    )(*os, *stats, hg, side, hnw, x2, pre_nw, wg, expand, wa, wh, wo, nw)


FFN_CHUNK = 256
FFN_SUB = 512
DOWN_PIECE = 256
CONV_ROWS = 64
HALO = SUBLANES
UBUF_SLOTS = 4


def _ffn_kernel(x_ref, nw1_ref, wup_ref, cw_ref, cb_ref, wdn_ref, nw2_ref, out_ref,
                ubuf, carry, act, hbuf, *, tm, tiles_per_seq):
    sub = FFN_SUB
    n_sub = tm // sub
    n_chunks = D_FF // FFN_CHUNK
    n_pieces = D_MODEL // DOWN_PIECE
    first = (pl.program_id(0) % tiles_per_seq) == 0

    @pl.when(pl.program_id(0) == 0)
    def _():
        carry[...] = jnp.zeros_like(carry)

    slabs = FFN_CHUNK // LANES

    def up_chunk(s, c):
        for half in range(2):
            col0 = half * D_FF + c * FFN_CHUNK
            u = jnp.dot(hbuf[s], wup_ref[:, col0:col0 + FFN_CHUNK], preferred_element_type=F32)
            for sl in range(slabs):
                slot = (2 * c + half) * slabs + sl
                buf = ubuf.at[c % UBUF_SLOTS, half, sl]
                prev = carry[slot]
                buf[0:HALO, :] = jnp.where(first, 0.0, prev) if s == 0 else prev
                buf[HALO:HALO + sub, :] = u[:, sl * LANES:(sl + 1) * LANES]
                carry[slot] = buf[sub:HALO + sub, :]
        for sl in range(slabs):
            for rb in range(sub // CONV_ROWS):
                conv = []
                for half in range(2):
                    col0 = half * D_FF + c * FFN_CHUNK + sl * LANES
                    buf = ubuf.at[c % UBUF_SLOTS, half, sl]
                    w = cw_ref[:, col0:col0 + LANES]
                    r0 = HALO + rb * CONV_ROWS
                    conv.append(cb_ref[:, col0:col0 + LANES]
                                + w[0:1] * buf[r0 - 2:r0 - 2 + CONV_ROWS, :]
                                + w[1:2] * buf[r0 - 1:r0 - 1 + CONV_ROWS, :]
                                + w[2:3] * buf[r0:r0 + CONV_ROWS, :])
                gate, val = conv
                gelu = 0.5 * gate * (1.0 + lax.erf(gate * (2.0 ** -0.5)))
                act[s, rb * CONV_ROWS:(rb + 1) * CONV_ROWS,
                    c * FFN_CHUNK + sl * LANES:c * FFN_CHUNK + (sl + 1) * LANES] = (gelu * val).astype(BF16)

    piece_after_chunk = {(n_chunks * (p + 1)) // n_pieces - 1: p for p in range(n_pieces)}

    for s in range(n_sub + 1):
        if s < n_sub:
            hbuf[s] = _rms(x_ref[s * sub:(s + 1) * sub, :], nw1_ref[...]).astype(BF16)
        pieces = []
        for c in range(n_chunks):
            if s < n_sub:
                up_chunk(s, c)
            if s > 0 and c in piece_after_chunk:
                p = piece_after_chunk[c]
                pieces.append(jnp.dot(act[s - 1], wdn_ref[:, p * DOWN_PIECE:(p + 1) * DOWN_PIECE],
                                      preferred_element_type=F32))
        if s > 0:
            rows = slice((s - 1) * sub, s * sub)
            y = jnp.concatenate(pieces, axis=1)
            out_ref[rows, :] = x_ref[rows, :] + _rms(y, nw2_ref[...])


def _ffn(x2, nw1, wup, cw, cb, wdn, nw2, seq, tm):
    n = x2.shape[0]
    row = pl.BlockSpec((tm, D_MODEL), lambda i: (i, 0))
    return pl.pallas_call(
        functools.partial(_ffn_kernel, tm=tm, tiles_per_seq=seq // tm),
        grid=(n // tm,),
        in_specs=[row, _resident((1, D_MODEL)), _resident((D_MODEL, 2 * D_FF)),
                  _resident((CONV_WIDTH, 2 * D_FF)), _resident((1, 2 * D_FF)),
                  _resident((D_FF, D_MODEL)), _resident((1, D_MODEL))],
        out_specs=row,
        out_shape=jax.ShapeDtypeStruct((n, D_MODEL), F32),
        scratch_shapes=[pltpu.VMEM((UBUF_SLOTS, 2, FFN_CHUNK // LANES, HALO + FFN_SUB, LANES), F32),
                        pltpu.VMEM((2 * D_FF // LANES, HALO, LANES), F32),
                        pltpu.VMEM((tm // FFN_SUB, FFN_SUB, D_FF), BF16),
                        pltpu.VMEM((tm // FFN_SUB, FFN_SUB, D_MODEL), BF16)],
        compiler_params=pltpu.CompilerParams(dimension_semantics=("arbitrary",),
                                             vmem_limit_bytes=VMEM_LIMIT),
        name="conv_ffn",
    )(x2, nw1, wup, cw, cb, wdn, nw2)


def kernel(x, pre_mix_norm, w_in, rel_bias, hgrn_lb_raw, hgrn_norm, w_branch_attn, w_branch_hgrn,
           w_out, post_mix_norm, pre_ffn_norm, w_up, conv_w, conv_b, w_down, post_ffn_norm):
    bsz, seq, d_model = x.shape
    depth = w_in.shape[0]
    n = bsz * seq
    assert d_model == D_MODEL and w_in.shape[1:] == (D_MODEL, IN_WIDTH) and w_up.shape[1:] == (D_MODEL, 2 * D_FF)
    assert rel_bias.shape == (NUM_BUCKETS, N_GROUPS * HEADS_PER_GROUP)
    assert seq % (max(ATTN_DILATIONS) * ATTN_BLOCK) == 0 and seq % ROW_TILE == 0 and seq % IN_ROW_TILE == 0
    lbs = jnp.cumsum(jax.nn.softmax(hgrn_lb_raw.astype(F32), axis=0), axis=0)
    x2 = x.reshape(n, D_MODEL)
    bias = _attn_bias(rel_bias)
    for l in range(depth):
        *qkvs, hg, side = _in_proj(
            x2, pre_mix_norm[l][None], w_in[l, :, :GATE_COL0].astype(BF16), lbs[l][None], bsz, seq,
            tm=IN_ROW_TILE)
        os, stats = [], []
        for g in range(N_GROUPS):
            o, stat = _attention_group(qkvs[g], bias, g)
            os.append(o)
            stats.append(stat)
        x2 = _mix(os, stats, hg, side, hgrn_norm[l][None], x2,
                  pre_mix_norm[l][None], w_in[l, :, GATE_COL0:].astype(BF16),
                  w_branch_attn[l].astype(BF16), w_branch_hgrn[l].astype(BF16),
                  w_out[l].astype(BF16), post_mix_norm[l][None], seq, tm=ROW_TILE)
        x2 = _ffn(x2, pre_ffn_norm[l][None], w_up[l].astype(BF16), conv_w[l], conv_b[l][None],
                  w_down[l].astype(BF16), post_ffn_norm[l][None], seq, tm=ROW_TILE)
    return x2.reshape(bsz, seq, D_MODEL)
```
